```python
import jax
import jax.numpy as jnp
from jax import lax
import numpy as np

D_MODEL = 2048
BATCH = 16
SEQ = 256
DEPTH = 4
DEC_BATCH = 2
DEC_SEQ = 2048
PAST_LEN = 512

GRID_W = 64
N_MIXERS = 4
N_MLSTM = (DEPTH + 3) // 4
N_NAT = (DEPTH + 2) // 4
N_GQA = (DEPTH + 1) // 4
N_SSD = DEPTH // 4
N_DENSE = (DEPTH + 1) // 2
N_MOE = DEPTH // 2

MLSTM_HEADS = 8
MLSTM_DV = D_MODEL // MLSTM_HEADS
MLSTM_DQK = MLSTM_DV // 2
MLSTM_CHUNK = 64
F_GATE_BIAS = 3.0

NAT_HEADS = 16
NAT_HD = D_MODEL // NAT_HEADS
NAT_KH = 8
NAT_KW = 16

GQA_HEADS = 16
GQA_KV_HEADS = 4
GQA_HD = D_MODEL // GQA_HEADS
ROPE_THETA = 10000.0
Q_BLOCK = 128

SSD_D_INNER = 2 * D_MODEL
SSD_HEADDIM = 64
SSD_HEADS = SSD_D_INNER // SSD_HEADDIM
SSD_GROUPS = 8
SSD_STATE = 128
SSD_CONV = 5
SSD_CHUNK = 128

D_FF = 7 * D_MODEL // 2
N_EXPERTS = 8
TOP_K = 2
D_FF_EXPERT = 7 * D_MODEL // 2
MOE_BLOCK = 256

EPS = 1e-6
NEG = -1e30

kernel_name = 'hybrid_mlstm_nat_gqa_ssd_diffusion_step'


def rmsnorm(x, g):
    xf = x.astype(jnp.float32)
    y = xf * lax.rsqrt(jnp.mean(xf * xf, axis=-1, keepdims=True) + EPS)
    return (y * g.astype(jnp.float32)).astype(x.dtype)


def modulation(cvec, w, b):
    return jnp.split(jax.nn.silu(cvec) @ w + b, 6, axis=-1)


def swiglu(x, w_in, w_out):
    g, up = jnp.split(x @ w_in, 2, axis=-1)
    return (jax.nn.silu(g) * up) @ w_out


def moe_swiglu(u, w_router, b_router, w_in, w_out):
    shape = u.shape
    x = u.reshape(-1, shape[-1])
    t = x.shape[0]
    logits = (x @ w_router).astype(jnp.float32) + b_router.astype(jnp.float32)
    top_logit, top_idx = lax.top_k(logits, TOP_K)
    top_w = jax.nn.softmax(top_logit, axis=-1)
    n_assign = t * TOP_K
    expert = top_idx.reshape(-1)
    token = jnp.repeat(jnp.arange(t, dtype=jnp.int32), TOP_K)
    weight = top_w.reshape(-1)
    order = jnp.argsort(expert)
    e_s, tok_s, w_s = expert[order], token[order], weight[order]
    counts = jnp.bincount(expert, length=N_EXPERTS)
    padded = (counts + MOE_BLOCK - 1) // MOE_BLOCK * MOE_BLOCK
    pad_end = jnp.cumsum(padded)
    pad_start = pad_end - padded
    sort_start = jnp.cumsum(counts) - counts
    pos = pad_start[e_s] + jnp.arange(n_assign, dtype=jnp.int32) - sort_start[e_s]
    n_blocks = -(-n_assign // MOE_BLOCK) + N_EXPERTS
    n_rows = n_blocks * MOE_BLOCK
    row_tok = jnp.full((n_rows,), t, jnp.int32).at[pos].set(tok_s)
    row_w = jnp.zeros((n_rows,), x.dtype).at[pos].set(w_s.astype(x.dtype))
    blk_expert = jnp.minimum(jnp.searchsorted(pad_end, jnp.arange(n_blocks, dtype=jnp.int32) * MOE_BLOCK, side='right'), N_EXPERTS - 1)
    x_rows = jnp.concatenate([x, jnp.zeros((1, x.shape[1]), x.dtype)], axis=0)[row_tok]

    def expert_block(args):
        xb, eb = args
        g, up = jnp.split(xb @ w_in[eb], 2, axis=-1)
        return (jax.nn.silu(g) * up) @ w_out[eb]

    y_rows = lax.map(expert_block, (x_rows.reshape(n_blocks, MOE_BLOCK, -1), blk_expert))
    y = jnp.zeros_like(x_rows).at[row_tok].add(y_rows.reshape(n_rows, -1) * row_w[:, None])[:t]
    return y.reshape(shape)


def blocked_attention(q, k, v):
    b, nq, hk, g, hd = q.shape
    nb = nq // Q_BLOCK
    qb = jnp.moveaxis(q.reshape(b, nb, Q_BLOCK, hk, g, hd), 1, 0)
    scale = hd ** -0.5

    def one_block(qi):
        s = jnp.einsum('bqhgd,bkhd->bhgqk', qi, k).astype(jnp.float32) * scale
        p = jax.nn.softmax(s, axis=-1).astype(v.dtype)
        return jnp.einsum('bhgqk,bkhd->bqhgd', p, v)

    o = lax.map(one_block, qb)
    return jnp.moveaxis(o, 0, 1).reshape(b, nq, hk * g * hd)


def axial_rope_tables(n, hd):
    t = jnp.arange(n)
    row = (t // GRID_W).astype(jnp.float32)
    col = (t % GRID_W).astype(jnp.float32)
    half = hd // 2
    freqs = ROPE_THETA ** (-jnp.arange(0, half, 2, dtype=jnp.float32) / half)
    ar = row[:, None] * freqs
    ac = col[:, None] * freqs
    cos = jnp.concatenate([jnp.cos(ar), jnp.cos(ar), jnp.cos(ac), jnp.cos(ac)], axis=-1)
    sin = jnp.concatenate([jnp.sin(ar), jnp.sin(ar), jnp.sin(ac), jnp.sin(ac)], axis=-1)
    return cos, sin


def apply_rope(x, cos, sin):
    shape = (x.shape[1],) + (1,) * (x.ndim - 3) + (x.shape[-1],)
    cos = cos.reshape(shape)
    sin = sin.reshape(shape)
    x1, x2, x3, x4 = jnp.split(x, 4, axis=-1)
    rot = jnp.concatenate([-x2, x1, -x4, x3], axis=-1)
    return (x * cos + rot * sin).astype(x.dtype)


def mlstm_zero_state(b):
    h = MLSTM_HEADS
    return (jnp.zeros((b, 2, h, MLSTM_DQK, MLSTM_DV), jnp.float32),
            jnp.zeros((b, 2, h, MLSTM_DQK), jnp.float32),
            jnp.full((b, 2, h), NEG, jnp.float32))


def mlstm_chunk_scan(q, k, v, li, lf, c0, n0, m0):
    b, s, h, dqk = q.shape
    L = MLSTM_CHUNK
    nc = s // L

    def chunks(t):
        t = t.reshape((b, nc, L, h) + t.shape[3:])
        return jnp.moveaxis(jnp.moveaxis(t, 1, 0), 3, 2)

    tril = jnp.tril(jnp.ones((L, L), dtype=bool))
    scale = dqk ** -0.5

    def step(carry, inp):
        cs, ns, m = carry
        qc, kc, vc, lic, lfc = inp
        qc = qc * scale
        bcum = jnp.cumsum(lfc, axis=-1)
        dlog = jnp.where(tril, bcum[..., :, None] - bcum[..., None, :] + lic[..., None, :], -jnp.inf)
        inter = bcum + m[..., None]
        mj = jnp.maximum(inter, jnp.max(dlog, axis=-1))
        dw = jnp.exp(dlog - mj[..., None])
        iw = jnp.exp(inter - mj)
        sc = jnp.einsum('bhjd,bhsd->bhjs', qc, kc) * dw
        num = jnp.einsum('bhjs,bhsv->bhjv', sc, vc) + iw[..., None] * jnp.einsum('bhjd,bhdv->bhjv', qc, cs)
        den = jnp.sum(sc, axis=-1) + iw * jnp.einsum('bhjd,bhd->bhj', qc, ns)
        hc = num / jnp.maximum(jnp.abs(den), jnp.exp(-mj))[..., None]
        bl = bcum[..., -1]
        elog = bl[..., None] - bcum + lic
        carry_log = bl + m
        m_new = jnp.maximum(carry_log, jnp.max(elog, axis=-1))
        ew = jnp.exp(elog - m_new[..., None])
        cw = jnp.exp(carry_log - m_new)
        cs = cw[..., None, None] * cs + jnp.einsum('bhs,bhsd,bhsv->bhdv', ew, kc, vc)
        ns = cw[..., None] * ns + jnp.einsum('bhs,bhsd->bhd', ew, kc)
        return (cs, ns, m_new), hc

    (cf, nf, mf), hs = lax.scan(step, (c0, n0, m0), (chunks(q), chunks(k), chunks(v), chunks(li), chunks(lf)))
    hs = hs.transpose(1, 0, 3, 2, 4).reshape(b, s, h, v.shape[-1])
    return hs, (cf, nf, mf)


def mlstm_mixer(u, w_in, b_gate, g_head, w_out, c0, n0, m0):
    b, s, _ = u.shape
    h, dqk, dv = MLSTM_HEADS, MLSTM_DQK, MLSTM_DV
    c0, n0, m0 = c0.astype(jnp.float32), n0.astype(jnp.float32), m0.astype(jnp.float32)
    proj = u @ w_in
    q, k, v, o, g = jnp.split(proj, [h * dqk, 2 * h * dqk, 2 * h * dqk + h * dv, 2 * h * dqk + 2 * h * dv], axis=-1)
    q = q.reshape(b, s, h, dqk).astype(jnp.float32)
    k = k.reshape(b, s, h, dqk).astype(jnp.float32)
    v = v.reshape(b, s, h, dv).astype(jnp.float32)
    g = g.astype(jnp.float32) + b_gate.astype(jnp.float32)
    ig_f, fg_f, ig_b, fg_b = jnp.split(g, 4, axis=-1)
    h_f, (cf, nf, mf) = mlstm_chunk_scan(q, k, v, ig_f, jax.nn.log_sigmoid(fg_f), c0[:, 0], n0[:, 0], m0[:, 0])
    h_b, (cb, nb, mb) = mlstm_chunk_scan(q[:, ::-1], k[:, ::-1], v[:, ::-1], ig_b[:, ::-1],
                                         jax.nn.log_sigmoid(fg_b)[:, ::-1], c0[:, 1], n0[:, 1], m0[:, 1])
    hsum = rmsnorm(h_f + h_b[:, ::-1], g_head.reshape(h, dv))
    hsum = hsum.reshape(b, s, h * dv) * jax.nn.sigmoid(o.astype(jnp.float32))
    out = hsum.astype(u.dtype) @ w_out
    return out, (jnp.stack([cf, cb], 1), jnp.stack([nf, nb], 1), jnp.stack([mf, mb], 1))


def nat_qkv(u, w_qkv):
    b, n, _ = u.shape
    qkv = (u @ w_qkv).reshape(b, n, 3, NAT_HEADS, NAT_HD)
    return qkv[:, :, 0], qkv[:, :, 1], qkv[:, :, 2]


def nat_context(u, w_qkv, w_out):
    q, k, v = nat_qkv(u, w_qkv)
    o = blocked_attention(q[:, :, :, None], k, v)
    return o @ w_out, (k, v)


def nat_latent(u, w_qkv, rpb, w_out, k_ctx, v_ctx):
    b, n, _ = u.shape
    h, hd = NAT_HEADS, NAT_HD
    rows = n // GRID_W
    kh = min(NAT_KH, rows)
    q, k, v = nat_qkv(u, w_qkv)
    k_ctx = k_ctx.astype(k.dtype)
    v_ctx = v_ctx.astype(v.dtype)
    qg = q.reshape(b, rows, GRID_W, h, hd)
    kg = k.reshape(b, rows, GRID_W, h, hd)
    vg = v.reshape(b, rows, GRID_W, h, hd)
    r_idx = jnp.arange(rows)
    win_rows = jnp.clip(r_idx - kh // 2, 0, rows - kh)[:, None] + jnp.arange(kh)
    k_rows = kg[:, win_rows]
    v_rows = vg[:, win_rows]
    c_idx = jnp.arange(GRID_W)
    c_start = jnp.clip(c_idx - NAT_KW // 2, 0, GRID_W - NAT_KW)
    col_ok = (c_idx[None, :] >= c_start[:, None]) & (c_idx[None, :] < c_start[:, None] + NAT_KW)
    row_off = win_rows - r_idx[:, None] + NAT_KH - 1
    col_off = jnp.clip(c_idx[None, :] - c_idx[:, None], -(NAT_KW - 1), NAT_KW - 1) + NAT_KW - 1
    bias = rpb[:, row_off][:, :, :, col_off].transpose(0, 1, 3, 2, 4)
    scale = hd ** -0.5
    s_win = jnp.einsum('brqhd,brjkhd->bhrqjk', qg, k_rows).astype(jnp.float32) * scale + bias.astype(jnp.float32)
    s_win = jnp.where(col_ok[:, None, :], s_win, NEG)
    s_ctx = jnp.einsum('brqhd,bkhd->bhrqk', qg, k_ctx).astype(jnp.float32) * scale
    nwin = kh * GRID_W
    p = jax.nn.softmax(jnp.concatenate([s_win.reshape(b, h, rows, GRID_W, nwin), s_ctx], axis=-1), axis=-1).astype(v.dtype)
    p_win = p[..., :nwin].reshape(b, h, rows, GRID_W, kh, GRID_W)
    o = jnp.einsum('bhrqjk,brjkhd->brqhd', p_win, v_rows) + jnp.einsum('bhrqk,bkhd->brqhd', p[..., nwin:], v_ctx)
    return o.reshape(b, n, h * hd) @ w_out


def gqa_qkv(u, w_qkv, q_g, k_g):
    b, n, _ = u.shape
    g = GQA_HEADS // GQA_KV_HEADS
    q, k, v = jnp.split(u @ w_qkv, [GQA_HEADS * GQA_HD, (GQA_HEADS + GQA_KV_HEADS) * GQA_HD], axis=-1)
    q = rmsnorm(q.reshape(b, n, GQA_KV_HEADS, g, GQA_HD), q_g)
    k = rmsnorm(k.reshape(b, n, GQA_KV_HEADS, GQA_HD), k_g)
    return q, k, v.reshape(b, n, GQA_KV_HEADS, GQA_HD)


def gqa_context(u, w_qkv, q_g, k_g, w_out):
    q, k, v = gqa_qkv(u, w_qkv, q_g, k_g)
    return blocked_attention(q, k, v) @ w_out, (k, v)


def gqa_latent(u, w_qkv, q_g, k_g, w_out, k_ctx, v_ctx):
    q, k, v = gqa_qkv(u, w_qkv, q_g, k_g)
    cos, sin = axial_rope_tables(u.shape[1], GQA_HD)
    q = apply_rope(q, cos, sin)
    k = apply_rope(k, cos, sin)
    kk = jnp.concatenate([k, k_ctx.astype(k.dtype)], axis=1)
    vv = jnp.concatenate([v, v_ctx.astype(v.dtype)], axis=1)
    return blocked_attention(q, kk, vv) @ w_out


def depthwise_conv(x, w, bias):
    kw = w.shape[0]
    y = lax.conv_general_dilated(x, w[:, None, :].astype(x.dtype), window_strides=(1,),
                                 padding=[(kw // 2, kw // 2)], dimension_numbers=('NWC', 'WIO', 'NWC'),
                                 feature_group_count=x.shape[-1])
    return y + bias.astype(x.dtype)


def ssd_chunk_scan(x, dt, a, bm, cm, s0):
    b, s, h, p = x.shape
    g, n = bm.shape[2], bm.shape[3]
    r = h // g
    L = SSD_CHUNK
    nc = s // L
    xr = (x * dt[..., None]).reshape(b, nc, L, g, r, p)
    acum = jnp.cumsum((dt * a).reshape(b, nc, L, g, r), axis=2)
    br = bm.reshape(b, nc, L, g, n)
    cr = cm.reshape(b, nc, L, g, n)
    tril = jnp.tril(jnp.ones((L, L), dtype=bool))
    seg = acum[:, :, :, None] - acum[:, :, None]
    lmat = jnp.exp(jnp.where(tril[:, :, None, None], seg, -jnp.inf))
    cb = jnp.einsum('bclgn,bcsgn->bclsg', cr, br)
    y_diag = jnp.einsum('bclsg,bclsgr,bcsgrp->bclgrp', cb, lmat, xr)
    decay = jnp.exp(acum[:, :, -1:] - acum)
    states = jnp.einsum('bclgn,bclgr,bclgrp->bcgrpn', br, decay, xr)
    chunk_decay = jnp.exp(acum[:, :, -1])

    def step(carry, inp):
        st, dec = inp
        return dec[..., None, None] * carry + st, carry

    s_final, starts = lax.scan(step, s0.reshape(b, g, r, p, n),
                               (jnp.moveaxis(states, 1, 0), jnp.moveaxis(chunk_decay, 1, 0)))
    starts = jnp.moveaxis(starts, 0, 1)
    y_off = jnp.einsum('bclgn,bcgrpn,bclgr->bclgrp', cr, starts, jnp.exp(acum))
    return (y_diag + y_off).reshape(b, s, h, p), s_final.reshape(b, h, p, n)


def ssd_mixer(u, w_in, conv_w, conv_b, dt_bias, a_log, d_skip, g_norm, w_out, s0):
    b, s, _ = u.shape
    di, gn, h = SSD_D_INNER, SSD_GROUPS * SSD_STATE, SSD_HEADS
    proj = u @ w_in
    z, xbc, dt = jnp.split(proj, [di, 2 * di + 2 * gn], axis=-1)
    xbc = jax.nn.silu(depthwise_conv(xbc, conv_w, conv_b)).astype(jnp.float32)
    xs, bm, cm = jnp.split(xbc, [di, di + gn], axis=-1)
    xs = xs.reshape(b, s, h, SSD_HEADDIM)
    bm = bm.reshape(b, s, SSD_GROUPS, SSD_STATE)
    cm = cm.reshape(b, s, SSD_GROUPS, SSD_STATE)
    dt = jax.nn.softplus(dt.astype(jnp.float32).reshape(b, s, 2, h) + dt_bias.astype(jnp.float32))
    a = -jnp.exp(a_log.astype(jnp.float32))
    s0 = s0.astype(jnp.float32)
    y_f, sf = ssd_chunk_scan(xs, dt[:, :, 0], a[0], bm, cm, s0[:, 0])
    y_b, sb = ssd_chunk_scan(xs[:, ::-1], dt[:, ::-1, 1], a[1], bm[:, ::-1], cm[:, ::-1], s0[:, 1])
    y = y_f + y_b[:, ::-1] + d_skip.astype(jnp.float32)[:, None] * xs
    y = rmsnorm(y.reshape(b, s, di) * jax.nn.silu(z.astype(jnp.float32)), g_norm)
    return y.astype(u.dtype) @ w_out, jnp.stack([sf, sb], 1)


def setup_inputs(seed: int = 0) -> dict:
    key = jax.random.key(seed)
    keys = iter(jax.random.split(key, 64))

    def nrm(shape, scale=1.0):
        return jax.random.normal(next(keys), shape, jnp.float32) * scale

    def gain(shape):
        return 1.0 + nrm(shape, 0.05)

    def unif(shape, lo, hi):
        return jax.random.uniform(next(keys), shape, jnp.float32, lo, hi)

    d = D_MODEL
    ds = d ** -0.5
    mh = MLSTM_HEADS
    gate_offset = jnp.asarray(np.concatenate([np.zeros(mh), np.full(mh, F_GATE_BIAS)] * 2).astype(np.float32))
    mlstm_in = 2 * mh * MLSTM_DQK + 2 * mh * MLSTM_DV + 4 * mh
    gn = SSD_GROUPS * SSD_STATE
    ssd_in = 2 * SSD_D_INNER + 2 * gn + 2 * SSD_HEADS
    conv_ch = SSD_D_INNER + 2 * gn
    dt0 = jnp.exp(unif((N_SSD, 2, SSD_HEADS), float(np.log(1e-3)), float(np.log(1e-1))))
    return {
        'x_prompt': nrm((BATCH, SEQ, d)),
        'x_sample': nrm((DEC_BATCH, DEC_SEQ, d)),
        'state_mlstm_c': nrm((DEC_BATCH, N_MLSTM, 2, mh, MLSTM_DQK, MLSTM_DV), 0.5),
        'state_mlstm_n': nrm((DEC_BATCH, N_MLSTM, 2, mh, MLSTM_DQK), 0.5),
        'state_mlstm_m': nrm((DEC_BATCH, N_MLSTM, 2, mh)),
        'cache_nat_k': nrm((DEC_BATCH, N_NAT, PAST_LEN, NAT_HEADS, NAT_HD)),
        'cache_nat_v': nrm((DEC_BATCH, N_NAT, PAST_LEN, NAT_HEADS, NAT_HD)),
        'cache_gqa_k': nrm((DEC_BATCH, N_GQA, PAST_LEN, GQA_KV_HEADS, GQA_HD)),
        'cache_gqa_v': nrm((DEC_BATCH, N_GQA, PAST_LEN, GQA_KV_HEADS, GQA_HD)),
        'state_ssd': nrm((DEC_BATCH, N_SSD, 2, SSD_HEADS, SSD_HEADDIM, SSD_STATE), 0.5),
        'c': nrm((DEC_BATCH, d)),
        'c_ctx': nrm((d,)),
        'w_mod': nrm((DEPTH, d, 6 * d), 0.5 * ds),
        'b_mod': nrm((DEPTH, 6 * d), 0.02),
        'norm_g': gain((DEPTH, 4, d)),
        'mlstm_w_in': nrm((N_MLSTM, d, mlstm_in), ds),
        'mlstm_b_gate': nrm((N_MLSTM, 4 * mh), 0.1) + gate_offset,
        'mlstm_g_head': gain((N_MLSTM, mh * MLSTM_DV)),
        'mlstm_w_out': nrm((N_MLSTM, mh * MLSTM_DV, d), (mh * MLSTM_DV) ** -0.5),
        'nat_w_qkv': nrm((N_NAT, d, 3 * NAT_HEADS * NAT_HD), ds),
        'nat_rpb': nrm((N_NAT, NAT_HEADS, 2 * NAT_KH - 1, 2 * NAT_KW - 1), 0.2),
        'nat_w_out': nrm((N_NAT, NAT_HEADS * NAT_HD, d), (NAT_HEADS * NAT_HD) ** -0.5),
        'gqa_w_qkv': nrm((N_GQA, d, (GQA_HEADS + 2 * GQA_KV_HEADS) * GQA_HD), ds),
        'gqa_q_g': gain((N_GQA, GQA_HD)),
        'gqa_k_g': gain((N_GQA, GQA_HD)),
        'gqa_w_out': nrm((N_GQA, GQA_HEADS * GQA_HD, d), (GQA_HEADS * GQA_HD) ** -0.5),
        'ssd_w_in': nrm((N_SSD, d, ssd_in), ds),
        'ssd_conv_w': nrm((N_SSD, SSD_CONV, conv_ch), SSD_CONV ** -0.5),
        'ssd_conv_b': nrm((N_SSD, conv_ch), 0.02),
        'ssd_dt_bias': dt0 + jnp.log(-jnp.expm1(-dt0)),
        'ssd_a_log': jnp.log(unif((N_SSD, 2, SSD_HEADS), 1.0, 16.0)),
        'ssd_d': 1.0 + nrm((N_SSD, SSD_HEADS), 0.1),
        'ssd_g_norm': gain((N_SSD, SSD_D_INNER)),
        'ssd_w_out': nrm((N_SSD, SSD_D_INNER, d), SSD_D_INNER ** -0.5),
        'ffn_w_in': nrm((N_DENSE, d, 2 * D_FF), ds),
        'ffn_w_out': nrm((N_DENSE, D_FF, d), D_FF ** -0.5),
        'moe_w_router': nrm((N_MOE, d, N_EXPERTS), ds),
        'moe_b_router': nrm((N_MOE, N_EXPERTS), 0.01),
        'moe_w_in': nrm((N_MOE, N_EXPERTS, d, 2 * D_FF_EXPERT), ds),
        'moe_w_out': nrm((N_MOE, N_EXPERTS, D_FF_EXPERT, d), D_FF_EXPERT ** -0.5),
    }


def reference(x_prompt, x_sample, state_mlstm_c, state_mlstm_n, state_mlstm_m,
              cache_nat_k, cache_nat_v, cache_gqa_k, cache_gqa_v, state_ssd, c,
              c_ctx, w_mod, b_mod, norm_g,
              mlstm_w_in, mlstm_b_gate, mlstm_g_head, mlstm_w_out,
              nat_w_qkv, nat_rpb, nat_w_out,
              gqa_w_qkv, gqa_q_g, gqa_k_g, gqa_w_out,
              ssd_w_in, ssd_conv_w, ssd_conv_b, ssd_dt_bias, ssd_a_log, ssd_d, ssd_g_norm, ssd_w_out,
              ffn_w_in, ffn_w_out, moe_w_router, moe_b_router, moe_w_in, moe_w_out):
    hp, hs = x_prompt, x_sample
    bp = x_prompt.shape[0]
    st_dtype = x_prompt.dtype
    mc, mn, mm, nk, nv, gk, gv, ss = [], [], [], [], [], [], [], []
    for layer in range(DEPTH):
        kind, j = layer % N_MIXERS, layer // N_MIXERS
        p_sh1, p_sc1, p_g1, p_sh2, p_sc2, p_g2 = modulation(c_ctx, w_mod[layer], b_mod[layer])
        s_sh1, s_sc1, s_g1, s_sh2, s_sc2, s_g2 = [t[:, None, :] for t in modulation(c, w_mod[layer], b_mod[layer])]
        up = rmsnorm(hp, norm_g[layer, 0]) * (1 + p_sc1) + p_sh1
        us = rmsnorm(hs, norm_g[layer, 0]) * (1 + s_sc1) + s_sh1
        if kind == 0:
            z_c, z_n, z_m = mlstm_zero_state(bp)
            op, (st_c, st_n, st_m) = mlstm_mixer(up, mlstm_w_in[j], mlstm_b_gate[j], mlstm_g_head[j], mlstm_w_out[j], z_c, z_n, z_m)
            os_, _ = mlstm_mixer(us, mlstm_w_in[j], mlstm_b_gate[j], mlstm_g_head[j], mlstm_w_out[j],
                                 state_mlstm_c[:, j], state_mlstm_n[:, j], state_mlstm_m[:, j])
            mc.append(st_c.astype(st_dtype))
            mn.append(st_n.astype(st_dtype))
            mm.append(st_m.astype(st_dtype))
        elif kind == 1:
            op, (k_ctx, v_ctx) = nat_context(up, nat_w_qkv[j], nat_w_out[j])
            os_ = nat_latent(us, nat_w_qkv[j], nat_rpb[j], nat_w_out[j], cache_nat_k[:, j], cache_nat_v[:, j])
            nk.append(k_ctx)
            nv.append(v_ctx)
        elif kind == 2:
            op, (k_ctx, v_ctx) = gqa_context(up, gqa_w_qkv[j], gqa_q_g[j], gqa_k_g[j], gqa_w_out[j])
            os_ = gqa_latent(us, gqa_w_qkv[j], gqa_q_g[j], gqa_k_g[j], gqa_w_out[j], cache_gqa_k[:, j], cache_gqa_v[:, j])
            gk.append(k_ctx)
            gv.append(v_ctx)
        else:
            z_s = jnp.zeros((bp, 2, SSD_HEADS, SSD_HEADDIM, SSD_STATE), jnp.float32)
            op, st_s = ssd_mixer(up, ssd_w_in[j], ssd_conv_w[j], ssd_conv_b[j], ssd_dt_bias[j], ssd_a_log[j],
                                 ssd_d[j], ssd_g_norm[j], ssd_w_out[j], z_s)
            os_, _ = ssd_mixer(us, ssd_w_in[j], ssd_conv_w[j], ssd_conv_b[j], ssd_dt_bias[j], ssd_a_log[j],
                               ssd_d[j], ssd_g_norm[j], ssd_w_out[j], state_ssd[:, j])
            ss.append(st_s.astype(st_dtype))
        hp = hp + p_g1 * rmsnorm(op, norm_g[layer, 1])
        hs = hs + s_g1 * rmsnorm(os_, norm_g[layer, 1])
        up = rmsnorm(hp, norm_g[layer, 2]) * (1 + p_sc2) + p_sh2
        us = rmsnorm(hs, norm_g[layer, 2]) * (1 + s_sc2) + s_sh2
        if layer % 2 == 0:
            fp = swiglu(up, ffn_w_in[layer // 2], ffn_w_out[layer // 2])
            fs = swiglu(us, ffn_w_in[layer // 2], ffn_w_out[layer // 2])
        else:
            e = layer // 2
            fp = moe_swiglu(up, moe_w_router[e], moe_b_router[e], moe_w_in[e], moe_w_out[e])
            fs = moe_swiglu(us, moe_w_router[e], moe_b_router[e], moe_w_in[e], moe_w_out[e])
        hp = hp + p_g2 * rmsnorm(fp, norm_g[layer, 3])
        hs = hs + s_g2 * rmsnorm(fs, norm_g[layer, 3])
    new_mlstm_c = jnp.stack(mc, axis=1)
    new_mlstm_n = jnp.stack(mn, axis=1)
    new_mlstm_m = jnp.stack(mm, axis=1)
    new_nat_k = jnp.stack(nk, axis=1)
    new_nat_v = jnp.stack(nv, axis=1)
    new_gqa_k = jnp.stack(gk, axis=1)
    new_gqa_v = jnp.stack(gv, axis=1)
    new_ssd = jnp.stack(ss, axis=1)
    return (hp, hs, new_mlstm_c, new_mlstm_n, new_mlstm_m, new_nat_k, new_nat_v, new_gqa_k, new_gqa_v, new_ssd)
```

```python
import functools

import jax
import jax.numpy as jnp
from jax import lax
from jax.experimental import pallas as pl
from jax.experimental.pallas import tpu as pltpu

D_MODEL = 2048
BATCH = 16
SEQ = 256
DEPTH = 4
DEC_BATCH = 2
DEC_SEQ = 2048
PAST_LEN = 512
GRID_W = 64
N_MIXERS = 4

MLSTM_HEADS = 8
MLSTM_DV = D_MODEL // MLSTM_HEADS
MLSTM_DQK = MLSTM_DV // 2
MLSTM_CHUNK = 64

NAT_HEADS = 16
NAT_HD = D_MODEL // NAT_HEADS
NAT_KH = 8
NAT_KW = 16

GQA_HEADS = 16
GQA_KV_HEADS = 4
GQA_HD = D_MODEL // GQA_HEADS
ROPE_THETA = 10000.0

SSD_D_INNER = 2 * D_MODEL
SSD_HEADDIM = 64
SSD_HEADS = SSD_D_INNER // SSD_HEADDIM
SSD_GROUPS = 8
SSD_STATE = 128
SSD_CONV = 5
SSD_CHUNK = 128

D_FF = 7 * D_MODEL // 2
N_EXPERTS = 8
TOP_K = 2

EPS = 1e-6
NEG = -1e30

N_PROMPT = BATCH * SEQ
N_TOK = N_PROMPT + DEC_BATCH * DEC_SEQ
N_GROUPS = 1 + DEC_BATCH
LANE = 128
VMEM_LIMIT = 56 * 1024 * 1024

F32 = jnp.float32
BF16 = jnp.bfloat16


def _params(sem, vmem=VMEM_LIMIT):
    return pltpu.CompilerParams(dimension_semantics=sem, vmem_limit_bytes=vmem)


def _group_of_tile(i, tm):
    return jnp.maximum(i * tm // DEC_SEQ - (N_PROMPT // DEC_SEQ - 1), 0)


def _rms(x, g):
    return x * lax.rsqrt(jnp.mean(x * x, axis=-1, keepdims=True) + EPS) * g


def _mod_kernel(c_ref, w_ref, b_ref, o_ref):
    c = c_ref[...]
    s = (c * jax.nn.sigmoid(c)).astype(BF16)
    o_ref[0] = jnp.dot(s, w_ref[0].astype(BF16), preferred_element_type=F32) + b_ref[0]


def modulation_all(cvecs, w_mod, b_mod):
    tn = 1024
    n = w_mod.shape[-1]
    return pl.pallas_call(
        _mod_kernel,
        grid=(DEPTH, n // tn),
        in_specs=[pl.BlockSpec((8, D_MODEL), lambda l, j: (0, 0)),
                  pl.BlockSpec((1, D_MODEL, tn), lambda l, j: (l, 0, j)),
                  pl.BlockSpec((1, 1, tn), lambda l, j: (l, 0, j))],
        out_specs=pl.BlockSpec((1, 8, tn), lambda l, j: (l, 0, j)),
        out_shape=jax.ShapeDtypeStruct((DEPTH, 8, n), F32),
        compiler_params=_params(("parallel", "parallel")),
        name="modulation",
    )(cvecs, w_mod, b_mod.reshape(DEPTH, 1, n))


def _norm_linear_kernel(x_ref, g_ref, sc_ref, sh_ref, w_ref, o_ref, u_ref):
    @pl.when(pl.program_id(1) == 0)
    def _():
        u = _rms(x_ref[...], g_ref[...]) * (1.0 + sc_ref[0]) + sh_ref[0]
        u_ref[...] = u.astype(BF16)

    o_ref[...] = jnp.dot(u_ref[...], w_ref[...].astype(BF16), preferred_element_type=F32).astype(o_ref.dtype)


def norm_linear(h, g, scale, shift, w, *, n_out=None, col0=0, tm=1024, tn=512, out_dtype=F32, name="norm_linear"):
    m, d = h.shape
    n_out = w.shape[1] if n_out is None else n_out
    return pl.pallas_call(
        _norm_linear_kernel,
        grid=(m // tm, pl.cdiv(n_out, tn)),
        in_specs=[pl.BlockSpec((tm, d), lambda i, j: (i, 0)),
                  pl.BlockSpec((1, d), lambda i, j: (0, 0)),
                  pl.BlockSpec((1, 1, d), lambda i, j: (_group_of_tile(i, tm), 0, 0)),
                  pl.BlockSpec((1, 1, d), lambda i, j: (_group_of_tile(i, tm), 0, 0)),
                  pl.BlockSpec((d, tn), lambda i, j: (0, j + col0))],
        out_specs=pl.BlockSpec((tm, tn), lambda i, j: (i, j)),
        out_shape=jax.ShapeDtypeStruct((m, n_out), out_dtype),
        scratch_shapes=[pltpu.VMEM((tm, d), BF16)],
        compiler_params=_params(("parallel", "arbitrary")),
        name=name,
    )(h, g.reshape(1, d), scale, shift, w)


def _norm_swiglu_kernel(x_ref, g_ref, sc_ref, sh_ref, wg_ref, wu_ref, o_ref, u_ref):
    @pl.when(pl.program_id(1) == 0)
    def _():
        u = _rms(x_ref[...], g_ref[...]) * (1.0 + sc_ref[0]) + sh_ref[0]
        u_ref[...] = u.astype(BF16)

    u = u_ref[...]
    a = jnp.dot(u, wg_ref[...].astype(BF16), preferred_element_type=F32)
    b = jnp.dot(u, wu_ref[...].astype(BF16), preferred_element_type=F32)
    o_ref[...] = (a * jax.nn.sigmoid(a) * b).astype(o_ref.dtype)


def norm_swiglu_in(h, g, scale, shift, w_in, *, tm=1024, tn=256):
    m, d = h.shape
    f = w_in.shape[1] // 2
    nj = f // tn
    return pl.pallas_call(
        _norm_swiglu_kernel,
        grid=(m // tm, nj),
        in_specs=[pl.BlockSpec((tm, d), lambda i, j: (i, 0)),
                  pl.BlockSpec((1, d), lambda i, j: (0, 0)),
                  pl.BlockSpec((1, 1, d), lambda i, j: (_group_of_tile(i, tm), 0, 0)),
                  pl.BlockSpec((1, 1, d), lambda i, j: (_group_of_tile(i, tm), 0, 0)),
                  pl.BlockSpec((d, tn), lambda i, j: (0, j)),
                  pl.BlockSpec((d, tn), lambda i, j: (0, j + nj))],
        out_specs=pl.BlockSpec((tm, tn), lambda i, j: (i, j)),
        out_shape=jax.ShapeDtypeStruct((m, f), BF16),
        scratch_shapes=[pltpu.VMEM((tm, d), BF16)],
        compiler_params=_params(("parallel", "arbitrary")),
        name="norm_swiglu_in",
    )(h, g.reshape(1, d), scale, shift, w_in, w_in)


def _linear_out_kernel(a_ref, w_ref, h_ref, g_ref, gate_ref, o_ref, acc_ref):
    k = pl.program_id(1)

    @pl.when(k == 0)
    def _():
        acc_ref[...] = jnp.zeros_like(acc_ref)

    acc_ref[...] += jnp.dot(a_ref[...].astype(BF16), w_ref[...].astype(BF16), preferred_element_type=F32)

    @pl.when(k == pl.num_programs(1) - 1)
    def _():
        o_ref[...] = h_ref[...] + gate_ref[0] * _rms(acc_ref[...], g_ref[...])


def linear_out(a, w, h, g, gate, *, tm=512, tk=512, name="linear_out"):
    m, kdim = a.shape
    d = w.shape[1]
    return pl.pallas_call(
        _linear_out_kernel,
        grid=(m // tm, kdim // tk),
        in_specs=[pl.BlockSpec((tm, tk), lambda i, k: (i, k)),
                  pl.BlockSpec((tk, d), lambda i, k: (k, 0)),
                  pl.BlockSpec((tm, d), lambda i, k: (i, 0)),
                  pl.BlockSpec((1, d), lambda i, k: (0, 0)),
                  pl.BlockSpec((1, 1, d), lambda i, k: (_group_of_tile(i, tm), 0, 0))],
        out_specs=pl.BlockSpec((tm, d), lambda i, k: (i, 0)),
        out_shape=jax.ShapeDtypeStruct((m, d), F32),
        scratch_shapes=[pltpu.VMEM((tm, d), F32)],
        compiler_params=_params(("parallel", "arbitrary")),
        name=name,
    )(a, w, h, g.reshape(1, d), gate)


def _norm_residual_kernel(f_ref, h_ref, g_ref, gate_ref, o_ref):
    o_ref[...] = h_ref[...] + gate_ref[0] * _rms(f_ref[...], g_ref[...])


def norm_residual(f, h, g, gate, *, tm=512):
    m, d = h.shape
    return pl.pallas_call(
        _norm_residual_kernel,
        grid=(m // tm,),
        in_specs=[pl.BlockSpec((tm, d), lambda i: (i, 0)),
                  pl.BlockSpec((tm, d), lambda i: (i, 0)),
                  pl.BlockSpec((1, d), lambda i: (0, 0)),
                  pl.BlockSpec((1, 1, d), lambda i: (_group_of_tile(i, tm), 0, 0))],
        out_specs=pl.BlockSpec((tm, d), lambda i: (i, 0)),
        out_shape=jax.ShapeDtypeStruct((m, d), F32),
        compiler_params=_params(("parallel",)),
        name="norm_residual",
    )(f, h, g.reshape(1, d), gate)


def _norm_router_kernel(x_ref, g_ref, sc_ref, sh_ref, wr_ref, br_ref, u_ref, lg_ref):
    u = _rms(x_ref[...], g_ref[...]) * (1.0 + sc_ref[0]) + sh_ref[0]
    u_ref[...] = u.astype(BF16)
    lg_ref[...] = jnp.dot(u, wr_ref[...], preferred_element_type=F32,
                          precision=lax.Precision.HIGHEST) + br_ref[...]


def norm_router(h, g, scale, shift, w_router, b_router, *, tm=512):
    m, d = h.shape
    wr = jnp.zeros((d, LANE), F32).at[:, :N_EXPERTS].set(w_router)
    br = jnp.zeros((1, LANE), F32).at[0, :N_EXPERTS].set(b_router)
    u, lg = pl.pallas_call(
        _norm_router_kernel,
        grid=(m // tm,),
        in_specs=[pl.BlockSpec((tm, d), lambda i: (i, 0)),
                  pl.BlockSpec((1, d), lambda i: (0, 0)),
                  pl.BlockSpec((1, 1, d), lambda i: (_group_of_tile(i, tm), 0, 0)),
                  pl.BlockSpec((1, 1, d), lambda i: (_group_of_tile(i, tm), 0, 0)),
                  pl.BlockSpec((d, LANE), lambda i: (0, 0)),
                  pl.BlockSpec((1, LANE), lambda i: (0, 0))],
        out_specs=[pl.BlockSpec((tm, d), lambda i: (i, 0)),
                   pl.BlockSpec((tm, LANE), lambda i: (i, 0))],
        out_shape=[jax.ShapeDtypeStruct((m, d), BF16), jax.ShapeDtypeStruct((m, LANE), F32)],
        compiler_params=_params(("parallel",)),
        name="norm_router",
    )(h, g.reshape(1, d), scale, shift, wr, br)
    return u, lg[:, :N_EXPERTS]


MOE_TM = 512


def _moe_in_kernel(be_ref, nu_ref, x_ref, wg_ref, wu_ref, o_ref):
    @pl.when(pl.program_id(1) < nu_ref[0])
    def _():
        x = x_ref[...]
        a = jnp.dot(x, wg_ref[0].astype(BF16), preferred_element_type=F32)
        b = jnp.dot(x, wu_ref[0].astype(BF16), preferred_element_type=F32)
        o_ref[...] = (a * jax.nn.sigmoid(a) * b).astype(o_ref.dtype)


def moe_in(x_rows, blk_expert, n_used, w_in, *, tn=512):
    r, d = x_rows.shape
    f = w_in.shape[2] // 2
    nj = f // tn
    nb = r // MOE_TM

    def rb(b, nu):
        return jnp.minimum(b, nu[0] - 1)

    return pl.pallas_call(
        _moe_in_kernel,
        grid_spec=pltpu.PrefetchScalarGridSpec(
            num_scalar_prefetch=2,
            grid=(nj, nb),
            in_specs=[pl.BlockSpec((MOE_TM, d), lambda j, b, be, nu: (rb(b, nu), 0)),
                      pl.BlockSpec((1, d, tn), lambda j, b, be, nu: (be[rb(b, nu)], 0, j)),
                      pl.BlockSpec((1, d, tn), lambda j, b, be, nu: (be[rb(b, nu)], 0, j + nj))],
            out_specs=pl.BlockSpec((MOE_TM, tn), lambda j, b, be, nu: (rb(b, nu), j)),
        ),
        out_shape=jax.ShapeDtypeStruct((r, f), BF16),
        compiler_params=_params(("arbitrary", "arbitrary")),
        name="moe_in",
    )(blk_expert, n_used, x_rows, w_in, w_in)


def _moe_out_kernel(be_ref, nu_ref, a_ref, w_ref, rw_ref, o_ref):
    @pl.when(pl.program_id(1) < nu_ref[0])
    def _():
        y = jnp.dot(a_ref[...], w_ref[0].astype(BF16), preferred_element_type=F32)
        o_ref[...] = y * rw_ref[...]


def moe_out(act_rows, blk_expert, n_used, w_out, row_w, *, tn=256):
    r, f = act_rows.shape
    d = w_out.shape[2]
    nb = r // MOE_TM

    def rb(b, nu):
        return jnp.minimum(b, nu[0] - 1)

    return pl.pallas_call(
        _moe_out_kernel,
        grid_spec=pltpu.PrefetchScalarGridSpec(
            num_scalar_prefetch=2,
            grid=(d // tn, nb),
            in_specs=[pl.BlockSpec((MOE_TM, f), lambda j, b, be, nu: (rb(b, nu), 0)),
                      pl.BlockSpec((1, f, tn), lambda j, b, be, nu: (be[rb(b, nu)], 0, j)),
                      pl.BlockSpec((MOE_TM, 1), lambda j, b, be, nu: (rb(b, nu), 0))],
            out_specs=pl.BlockSpec((MOE_TM, tn), lambda j, b, be, nu: (rb(b, nu), j)),
        ),
        out_shape=jax.ShapeDtypeStruct((r, d), F32),
        compiler_params=_params(("arbitrary", "arbitrary")),
        name="moe_out",
    )(blk_expert, n_used, act_rows, w_out, row_w.reshape(r, 1))


def moe_layer(h, g, scale, shift, w_router, b_router, w_in, w_out, g_post, gate):
    t = h.shape[0]
    u, logits = norm_router(h, g, scale, shift, w_router, b_router)
    top_logit, top_idx = lax.top_k(logits, TOP_K)
    top_w = jax.nn.softmax(top_logit, axis=-1)
    expert = top_idx.reshape(-1)
    n_assign = t * TOP_K
    onehot = (expert[:, None] == jnp.arange(N_EXPERTS, dtype=jnp.int32)[None, :]).astype(jnp.int32)
    rank = jnp.sum((jnp.cumsum(onehot, axis=0) - onehot) * onehot, axis=1)
    counts = jnp.sum(onehot, axis=0)
    padded = (counts + MOE_TM - 1) // MOE_TM * MOE_TM
    pad_end = jnp.cumsum(padded)
    pad_start = pad_end - padded
    pos = (pad_start[expert] + rank).astype(jnp.int32)
    n_blocks = n_assign // MOE_TM + N_EXPERTS
    n_rows = n_blocks * MOE_TM
    token = jnp.repeat(jnp.arange(t, dtype=jnp.int32), TOP_K)
    row_tok = jnp.full((n_rows,), t, jnp.int32).at[pos].set(token)
    row_w = jnp.zeros((n_rows,), F32).at[pos].set(top_w.reshape(-1))
    blk_expert = jnp.minimum(
        jnp.searchsorted(pad_end, jnp.arange(n_blocks, dtype=jnp.int32) * MOE_TM, side='right'),
        N_EXPERTS - 1).astype(jnp.int32)
    n_used = (pad_end[-1] // MOE_TM).astype(jnp.int32).reshape(1)
    x_rows = jnp.concatenate([u, jnp.zeros((1, u.shape[1]), u.dtype)], axis=0)[row_tok]
    act = moe_in(x_rows, blk_expert, n_used, w_in)
    y_rows = moe_out(act, blk_expert, n_used, w_out, row_w)
    pos2 = pos.reshape(t, TOP_K)
    y = y_rows[pos2[:, 0]] + y_rows[pos2[:, 1]]
    return norm_residual(y, h, g_post, gate)


def _attn_kernel(*refs, scale, two):
    if two:
        q_ref, k_ref, v_ref, kc_ref, vc_ref, o_ref = refs
    else:
        q_ref, k_ref, v_ref, o_ref = refs
    nt = (((1,), (1,)), ((), ()))
    q = q_ref[...].astype(BF16)
    s = lax.dot_general(q, k_ref[...].astype(BF16), nt, preferred_element_type=F32) * scale
    m = jnp.max(s, axis=-1, keepdims=True)
    if two:
        s2 = lax.dot_general(q, kc_ref[0].astype(BF16), nt, preferred_element_type=F32) * scale
        m = jnp.maximum(m, jnp.max(s2, axis=-1, keepdims=True))
    p = jnp.exp(s - m)
    l = jnp.sum(p, axis=-1, keepdims=True)
    o = jnp.dot(p.astype(BF16), v_ref[...].astype(BF16), preferred_element_type=F32)
    if two:
        p2 = jnp.exp(s2 - m)
        l = l + jnp.sum(p2, axis=-1, keepdims=True)
        o = o + jnp.dot(p2.astype(BF16), vc_ref[0].astype(BF16), preferred_element_type=F32)
    o_ref[...] = (o / l).astype(o_ref.dtype)


def attention(q_arr, k_arr, v_arr, *, n_batch, n_heads, nq, nk, tq, row0, q_col, k_col, v_col, out_rows,
              k_cache=None, v_cache=None, kv_of_head=lambda h: h, name="attention"):
    hd = LANE
    nqb = nq // tq
    two = k_cache is not None
    in_specs = [pl.BlockSpec((tq, hd), lambda b, h, i: (row0 // tq + b * nqb + i, q_col(h))),
                pl.BlockSpec((nk, hd), lambda b, h, i: (row0 // nk + b, k_col(h))),
                pl.BlockSpec((nk, hd), lambda b, h, i: (row0 // nk + b, v_col(h)))]
    args = [q_arr, k_arr, v_arr]
    if two:
        n_past = k_cache.shape[1]
        in_specs += [pl.BlockSpec((1, n_past, hd), lambda b, h, i: (b, 0, kv_of_head(h))),
                     pl.BlockSpec((1, n_past, hd), lambda b, h, i: (b, 0, kv_of_head(h)))]
        args += [k_cache, v_cache]
    return pl.pallas_call(
        functools.partial(_attn_kernel, scale=hd ** -0.5, two=two),
        grid=(n_batch, n_heads, nqb),
        in_specs=in_specs,
        out_specs=pl.BlockSpec((tq, hd), lambda b, h, i: (b * nqb + i, h)),
        out_shape=jax.ShapeDtypeStruct((out_rows, n_heads * hd), BF16),
        compiler_params=_params(("parallel", "parallel", "arbitrary")),
        name=name,
    )(*args)


def _qk_norm_rope_kernel(x_ref, g_ref, cos_ref, sa_ref, sb_ref, o_ref, *, n_heads):
    cos, sa, sb = cos_ref[...], sa_ref[...], sb_ref[...]
    quarter = GQA_HD // 4
    for hh in range(n_heads):
        sl = slice(hh * GQA_HD, (hh + 1) * GQA_HD)
        y = _rms(x_ref[:, sl], g_ref[:, sl])
        up = pltpu.roll(y, GQA_HD - quarter, axis=1)
        dn = pltpu.roll(y, quarter, axis=1)
        o_ref[:, sl] = y * cos + up * sa + dn * sb


def qk_norm_rope(qkv, gains, cos, sa, sb, *, n_heads, tm=512):
    m = qkv.shape[0]
    w = n_heads * GQA_HD
    return pl.pallas_call(
        functools.partial(_qk_norm_rope_kernel, n_heads=n_heads),
        grid=(m // tm,),
        in_specs=[pl.BlockSpec((tm, w), lambda i: (i, 0)),
                  pl.BlockSpec((1, w), lambda i: (0, 0)),
                  pl.BlockSpec((tm, GQA_HD), lambda i: (i, 0)),
                  pl.BlockSpec((tm, GQA_HD), lambda i: (i, 0)),
                  pl.BlockSpec((tm, GQA_HD), lambda i: (i, 0))],
        out_specs=pl.BlockSpec((tm, w), lambda i: (i, 0)),
        out_shape=jax.ShapeDtypeStruct((m, w), F32),
        compiler_params=_params(("parallel",)),
        name="qk_norm_rope",
    )(qkv, gains, cos, sa, sb)


def rope_tables():
    t = jnp.arange(DEC_SEQ)
    row = (t // GRID_W).astype(F32)
    col = (t % GRID_W).astype(F32)
    half = GQA_HD // 2
    freqs = ROPE_THETA ** (-jnp.arange(0, half, 2, dtype=F32) / half)
    ar = row[:, None] * freqs
    ac = col[:, None] * freqs
    z = jnp.zeros_like(ar)
    cos = jnp.concatenate([jnp.cos(ar), jnp.cos(ar), jnp.cos(ac), jnp.cos(ac)], axis=-1)
    sa = jnp.concatenate([-jnp.sin(ar), z, -jnp.sin(ac), z], axis=-1)
    sb = jnp.concatenate([z, jnp.sin(ar), z, jnp.sin(ac)], axis=-1)
    ones = jnp.ones((N_PROMPT, GQA_HD), F32)
    zeros = jnp.zeros((N_PROMPT, GQA_HD), F32)
    tile = lambda a: jnp.tile(a, (DEC_BATCH, 1))
    return (jnp.concatenate([ones, tile(cos)], 0), jnp.concatenate([zeros, tile(sa)], 0),
            jnp.concatenate([zeros, tile(sb)], 0))


def _rmsnorm(x, g):
    xf = x.astype(F32)
    y = xf * lax.rsqrt(jnp.mean(xf * xf, axis=-1, keepdims=True) + EPS)
    return (y * g.astype(F32)).astype(x.dtype)


def mlstm_chunk_scan(q, k, v, li, lf, c0, n0, m0):
    b, s, h, dqk = q.shape
    L = MLSTM_CHUNK
    nc = s // L

    def chunks(t):
        t = t.reshape((b, nc, L, h) + t.shape[3:])
        return jnp.moveaxis(jnp.moveaxis(t, 1, 0), 3, 2)

    tril = jnp.tril(jnp.ones((L, L), dtype=bool))
    scale = dqk ** -0.5

    def step(carry, inp):
        cs, ns, m = carry
        qc, kc, vc, lic, lfc = inp
        qc = qc * scale
        bcum = jnp.cumsum(lfc, axis=-1)
        dlog = jnp.where(tril, bcum[..., :, None] - bcum[..., None, :] + lic[..., None, :], -jnp.inf)
        inter = bcum + m[..., None]
        mj = jnp.maximum(inter, jnp.max(dlog, axis=-1))
        dw = jnp.exp(dlog - mj[..., None])
        iw = jnp.exp(inter - mj)
        sc = jnp.einsum('bhjd,bhsd->bhjs', qc, kc) * dw
        num = jnp.einsum('bhjs,bhsv->bhjv', sc, vc) + iw[..., None] * jnp.einsum('bhjd,bhdv->bhjv', qc, cs)
        den = jnp.sum(sc, axis=-1) + iw * jnp.einsum('bhjd,bhd->bhj', qc, ns)
        hc = num / jnp.maximum(jnp.abs(den), jnp.exp(-mj))[..., None]
        bl = bcum[..., -1]
        elog = bl[..., None] - bcum + lic
        carry_log = bl + m
        m_new = jnp.maximum(carry_log, jnp.max(elog, axis=-1))
        ew = jnp.exp(elog - m_new[..., None])
        cw = jnp.exp(carry_log - m_new)
        cs = cw[..., None, None] * cs + jnp.einsum('bhs,bhsd,bhsv->bhdv', ew, kc, vc)
        ns = cw[..., None] * ns + jnp.einsum('bhs,bhsd->bhd', ew, kc)
        return (cs, ns, m_new), hc

    (cf, nf, mf), hs = lax.scan(step, (c0, n0, m0), (chunks(q), chunks(k), chunks(v), chunks(li), chunks(lf)))
    hs = hs.transpose(1, 0, 3, 2, 4).reshape(b, s, h, v.shape[-1])
    return hs, (cf, nf, mf)


def mlstm_seq(proj, gates, b_gate, g_head, c0, n0, m0):
    b, s, _ = proj.shape
    h, dqk, dv = MLSTM_HEADS, MLSTM_DQK, MLSTM_DV
    q, k, v, o = jnp.split(proj, [h * dqk, 2 * h * dqk, 2 * h * dqk + h * dv], axis=-1)
    q = q.reshape(b, s, h, dqk)
    k = k.reshape(b, s, h, dqk)
    v = v.reshape(b, s, h, dv)
    g = gates + b_gate
    ig_f, fg_f, ig_b, fg_b = jnp.split(g, 4, axis=-1)
    h_f, (cf, nf, mf) = mlstm_chunk_scan(q, k, v, ig_f, jax.nn.log_sigmoid(fg_f), c0[:, 0], n0[:, 0], m0[:, 0])
    h_b, (cb, nb, mb) = mlstm_chunk_scan(q[:, ::-1], k[:, ::-1], v[:, ::-1], ig_b[:, ::-1],
                                         jax.nn.log_sigmoid(fg_b)[:, ::-1], c0[:, 1], n0[:, 1], m0[:, 1])
    hsum = _rmsnorm(h_f + h_b[:, ::-1], g_head.reshape(h, dv))
    hsum = hsum.reshape(b, s, h * dv) * jax.nn.sigmoid(o)
    return hsum, (jnp.stack([cf, cb], 1), jnp.stack([nf, nb], 1), jnp.stack([mf, mb], 1))


def nat_latent_seq(q, k, v, rpb, k_ctx, v_ctx):
    b, n, h, hd = q.shape
    rows = n // GRID_W
    kh = min(NAT_KH, rows)
    qg = q.reshape(b, rows, GRID_W, h, hd)
    kg = k.reshape(b, rows, GRID_W, h, hd)
    vg = v.reshape(b, rows, GRID_W, h, hd)
    r_idx = jnp.arange(rows)
    win_rows = jnp.clip(r_idx - kh // 2, 0, rows - kh)[:, None] + jnp.arange(kh)
    k_rows = kg[:, win_rows]
    v_rows = vg[:, win_rows]
    c_idx = jnp.arange(GRID_W)
    c_start = jnp.clip(c_idx - NAT_KW // 2, 0, GRID_W - NAT_KW)
    col_ok = (c_idx[None, :] >= c_start[:, None]) & (c_idx[None, :] < c_start[:, None] + NAT_KW)
    row_off = win_rows - r_idx[:, None] + NAT_KH - 1
    col_off = jnp.clip(c_idx[None, :] - c_idx[:, None], -(NAT_KW - 1), NAT_KW - 1) + NAT_KW - 1
    bias = rpb[:, row_off][:, :, :, col_off].transpose(0, 1, 3, 2, 4)
    scale = hd ** -0.5
    s_win = jnp.einsum('brqhd,brjkhd->bhrqjk', qg, k_rows).astype(F32) * scale + bias
    s_win = jnp.where(col_ok[:, None, :], s_win, NEG)
    s_ctx = jnp.einsum('brqhd,bkhd->bhrqk', qg, k_ctx).astype(F32) * scale
    nwin = kh * GRID_W
    p = jax.nn.softmax(jnp.concatenate([s_win.reshape(b, h, rows, GRID_W, nwin), s_ctx], axis=-1), axis=-1)
    p_win = p[..., :nwin].reshape(b, h, rows, GRID_W, kh, GRID_W)
    o = jnp.einsum('bhrqjk,brjkhd->brqhd', p_win, v_rows) + jnp.einsum('bhrqk,bkhd->brqhd', p[..., nwin:], v_ctx)
    return o.reshape(b, n, h * hd)


def depthwise_conv(x, w, bias):
    kw = w.shape[0]
    y = lax.conv_general_dilated(x, w[:, None, :].astype(x.dtype), window_strides=(1,),
                                 padding=[(kw // 2, kw // 2)], dimension_numbers=('NWC', 'WIO', 'NWC'),
                                 feature_group_count=x.shape[-1])
    return y + bias.astype(x.dtype)


def ssd_chunk_scan(x, dt, a, bm, cm, s0):
    b, s, h, p = x.shape
    g, n = bm.shape[2], bm.shape[3]
    r = h // g
    L = SSD_CHUNK
    nc = s // L
    xr = (x * dt[..., None]).reshape(b, nc, L, g, r, p)
    acum = jnp.cumsum((dt * a).reshape(b, nc, L, g, r), axis=2)
    br = bm.reshape(b, nc, L, g, n)
    cr = cm.reshape(b, nc, L, g, n)
    tril = jnp.tril(jnp.ones((L, L), dtype=bool))
    seg = acum[:, :, :, None] - acum[:, :, None]
    lmat = jnp.exp(jnp.where(tril[:, :, None, None], seg, -jnp.inf))
    cb = jnp.einsum('bclgn,bcsgn->bclsg', cr, br)
    y_diag = jnp.einsum('bclsg,bclsgr,bcsgrp->bclgrp', cb, lmat, xr)
    decay = jnp.exp(acum[:, :, -1:] - acum)
    states = jnp.einsum('bclgn,bclgr,bclgrp->bcgrpn', br, decay, xr)
    chunk_decay = jnp.exp(acum[:, :, -1])

    def step(carry, inp):
        st, dec = inp
        return dec[..., None, None] * carry + st, carry

    s_final, starts = lax.scan(step, s0.reshape(b, g, r, p, n),
                               (jnp.moveaxis(states, 1, 0), jnp.moveaxis(chunk_decay, 1, 0)))
    starts = jnp.moveaxis(starts, 0, 1)
    y_off = jnp.einsum('bclgn,bcgrpn,bclgr->bclgrp', cr, starts, jnp.exp(acum))
    return (y_diag + y_off).reshape(b, s, h, p), s_final.reshape(b, h, p, n)


def ssd_seq(proj, conv_w, conv_b, dt_bias, a_log, d_skip, g_norm, s0):
    b, s, _ = proj.shape
    di, gn, h = SSD_D_INNER, SSD_GROUPS * SSD_STATE, SSD_HEADS
    z, xbc, dt = jnp.split(proj, [di, 2 * di + 2 * gn], axis=-1)
    xbc = jax.nn.silu(depthwise_conv(xbc, conv_w, conv_b))
    xs, bm, cm = jnp.split(xbc, [di, di + gn], axis=-1)
    xs = xs.reshape(b, s, h, SSD_HEADDIM)
    bm = bm.reshape(b, s, SSD_GROUPS, SSD_STATE)
    cm = cm.reshape(b, s, SSD_GROUPS, SSD_STATE)
    dt = jax.nn.softplus(dt.reshape(b, s, 2, h) + dt_bias)
    a = -jnp.exp(a_log)
    y_f, sf = ssd_chunk_scan(xs, dt[:, :, 0], a[0], bm, cm, s0[:, 0])
    y_b, sb = ssd_chunk_scan(xs[:, ::-1], dt[:, ::-1, 1], a[1], bm[:, ::-1], cm[:, ::-1], s0[:, 1])
    y = y_f + y_b[:, ::-1] + d_skip[:, None] * xs
    y = _rmsnorm(y.reshape(b, s, di) * jax.nn.silu(z), g_norm)
    return y, jnp.stack([sf, sb], 1)


def kernel(x_prompt, x_sample, state_mlstm_c, state_mlstm_n, state_mlstm_m, cache_nat_k, cache_nat_v, cache_gqa_k, cache_gqa_v, state_ssd, c, c_ctx, w_mod, b_mod, norm_g, mlstm_w_in, mlstm_b_gate, mlstm_g_head, mlstm_w_out, nat_w_qkv, nat_rpb, nat_w_out, gqa_w_qkv, gqa_q_g, gqa_k_g, gqa_w_out, ssd_w_in, ssd_conv_w, ssd_conv_b, ssd_dt_bias, ssd_a_log, ssd_d, ssd_g_norm, ssd_w_out, ffn_w_in, ffn_w_out, moe_w_router, moe_b_router, moe_w_in, moe_w_out):
    d = D_MODEL
    h = jnp.concatenate([x_prompt.reshape(N_PROMPT, d), x_sample.reshape(DEC_BATCH * DEC_SEQ, d)], axis=0)
    cvecs = jnp.zeros((8, d), F32).at[0].set(c_ctx).at[1:1 + DEC_BATCH].set(c)
    mod = modulation_all(cvecs, w_mod, b_mod)[:, :N_GROUPS].reshape(DEPTH, N_GROUPS, 6, 1, d)
    outs = {}
    for layer in range(DEPTH):
        kind, j = layer % N_MIXERS, layer // N_MIXERS
        sh1, sc1, g1, sh2, sc2, g2 = [mod[layer, :, i] for i in range(6)]
        ng = norm_g[layer]
        if kind == 0:
            w_in = mlstm_w_in[j]
            n_main = 2 * MLSTM_HEADS * MLSTM_DQK + 2 * MLSTM_HEADS * MLSTM_DV
            proj = norm_linear(h, ng[0], sc1, sh1, w_in, n_out=n_main, name="mlstm_in")
            gates = norm_linear(h, ng[0], sc1, sh1, w_in, n_out=LANE, col0=n_main // LANE, tn=LANE,
                                name="mlstm_gates")[:, :4 * MLSTM_HEADS]
            bp = BATCH
            z_c = jnp.zeros((bp, 2, MLSTM_HEADS, MLSTM_DQK, MLSTM_DV), F32)
            z_n = jnp.zeros((bp, 2, MLSTM_HEADS, MLSTM_DQK), F32)
            z_m = jnp.full((bp, 2, MLSTM_HEADS), NEG, F32)
            a_p, (st_c, st_n, st_m) = mlstm_seq(proj[:N_PROMPT].reshape(BATCH, SEQ, -1),
                                                gates[:N_PROMPT].reshape(BATCH, SEQ, -1),
                                                mlstm_b_gate[j], mlstm_g_head[j], z_c, z_n, z_m)
            a_s, _ = mlstm_seq(proj[N_PROMPT:].reshape(DEC_BATCH, DEC_SEQ, -1),
                               gates[N_PROMPT:].reshape(DEC_BATCH, DEC_SEQ, -1),
                               mlstm_b_gate[j], mlstm_g_head[j],
                               state_mlstm_c[:, j], state_mlstm_n[:, j], state_mlstm_m[:, j])
            outs['mc'], outs['mn'], outs['mm'] = st_c[:, None], st_n[:, None], st_m[:, None]
            a = jnp.concatenate([a_p.reshape(N_PROMPT, -1), a_s.reshape(DEC_BATCH * DEC_SEQ, -1)], axis=0)
            h = linear_out(a, mlstm_w_out[j], h, ng[1], g1, name="mlstm_out")
        elif kind == 1:
            hw = NAT_HEADS * NAT_HD
            qkv = norm_linear(h, ng[0], sc1, sh1, nat_w_qkv[j], name="nat_qkv")
            nh = NAT_HEADS
            o_p = attention(qkv, qkv, qkv, n_batch=BATCH, n_heads=nh, nq=SEQ, nk=SEQ, tq=SEQ, row0=0,
                            q_col=lambda hh: hh, k_col=lambda hh: nh + hh, v_col=lambda hh: 2 * nh + hh,
                            out_rows=N_PROMPT, name="nat_ctx_attn")
            qkv_s = qkv[N_PROMPT:].reshape(DEC_BATCH, DEC_SEQ, 3, NAT_HEADS, NAT_HD)
            o_s = nat_latent_seq(qkv_s[:, :, 0], qkv_s[:, :, 1], qkv_s[:, :, 2], nat_rpb[j],
                                 cache_nat_k[:, j], cache_nat_v[:, j])
            outs['nk'] = qkv[:N_PROMPT, hw:2 * hw].reshape(BATCH, 1, SEQ, NAT_HEADS, NAT_HD)
            outs['nv'] = qkv[:N_PROMPT, 2 * hw:].reshape(BATCH, 1, SEQ, NAT_HEADS, NAT_HD)
            a = jnp.concatenate([o_p, o_s.reshape(DEC_BATCH * DEC_SEQ, hw).astype(BF16)], axis=0)
            h = linear_out(a, nat_w_out[j], h, ng[1], g1, name="nat_out")
        elif kind == 2:
            nqk = GQA_HEADS + GQA_KV_HEADS
            qkv = norm_linear(h, ng[0], sc1, sh1, gqa_w_qkv[j], name="gqa_qkv")
            gains = jnp.concatenate([jnp.tile(gqa_q_g[j], GQA_HEADS), jnp.tile(gqa_k_g[j], GQA_KV_HEADS)])
            cos, sa, sb = rope_tables()
            qk = qk_norm_rope(qkv, gains.reshape(1, -1), cos, sa, sb, n_heads=nqk)
            grp = GQA_HEADS // GQA_KV_HEADS
            kcol = lambda hh: GQA_HEADS + hh // grp
            vcol = lambda hh: nqk + GQA_KV_HEADS + hh // grp
            vcol_qkv = lambda hh: GQA_HEADS + GQA_KV_HEADS + hh // grp
            o_p = attention(qk, qk, qkv, n_batch=BATCH, n_heads=GQA_HEADS, nq=SEQ, nk=SEQ, tq=SEQ, row0=0,
                            q_col=lambda hh: hh, k_col=kcol, v_col=vcol_qkv, out_rows=N_PROMPT,
                            name="gqa_ctx_attn")
            kc = cache_gqa_k[:, j].reshape(DEC_BATCH, PAST_LEN, GQA_KV_HEADS * GQA_HD)
            vc = cache_gqa_v[:, j].reshape(DEC_BATCH, PAST_LEN, GQA_KV_HEADS * GQA_HD)
            o_s = attention(qk, qk, qkv, n_batch=DEC_BATCH, n_heads=GQA_HEADS, nq=DEC_SEQ, nk=DEC_SEQ, tq=256,
                            row0=N_PROMPT, q_col=lambda hh: hh, k_col=kcol, v_col=vcol_qkv,
                            out_rows=DEC_BATCH * DEC_SEQ, k_cache=kc, v_cache=vc,
                            kv_of_head=lambda hh: hh // grp, name="gqa_lat_attn")
            kw = GQA_KV_HEADS * GQA_HD
            outs['gk'] = qk[:N_PROMPT, GQA_HEADS * GQA_HD:].reshape(BATCH, 1, SEQ, GQA_KV_HEADS, GQA_HD)
            outs['gv'] = qkv[:N_PROMPT, GQA_HEADS * GQA_HD + kw:].reshape(BATCH, 1, SEQ, GQA_KV_HEADS, GQA_HD)
            a = jnp.concatenate([o_p, o_s], axis=0)
            h = linear_out(a, gqa_w_out[j], h, ng[1], g1, name="gqa_out")
        else:
            proj = norm_linear(h, ng[0], sc1, sh1, ssd_w_in[j], tn=384, name="ssd_in")
            z_s = jnp.zeros((BATCH, 2, SSD_HEADS, SSD_HEADDIM, SSD_STATE), F32)
            args = (ssd_conv_w[j], ssd_conv_b[j], ssd_dt_bias[j], ssd_a_log[j], ssd_d[j], ssd_g_norm[j])
            y_p, st_s = ssd_seq(proj[:N_PROMPT].reshape(BATCH, SEQ, -1), *args, z_s)
            y_s, _ = ssd_seq(proj[N_PROMPT:].reshape(DEC_BATCH, DEC_SEQ, -1), *args, state_ssd[:, j])
            outs['ss'] = st_s[:, None]
            a = jnp.concatenate([y_p.reshape(N_PROMPT, -1), y_s.reshape(DEC_BATCH * DEC_SEQ, -1)], axis=0)
            h = linear_out(a, ssd_w_out[j], h, ng[1], g1, name="ssd_out")
        if layer % 2 == 0:
            e = layer // 2
            act = norm_swiglu_in(h, ng[2], sc2, sh2, ffn_w_in[e])
            h = linear_out(act, ffn_w_out[e], h, ng[3], g2, name="ffn_out")
        else:
            e = layer // 2
            h = moe_layer(h, ng[2], sc2, sh2, moe_w_router[e], moe_b_router[e], moe_w_in[e], moe_w_out[e],
                          ng[3], g2)
    y_prompt = h[:N_PROMPT].reshape(BATCH, SEQ, d)
    y_sample = h[N_PROMPT:].reshape(DEC_BATCH, DEC_SEQ, d)
    return (y_prompt, y_sample, outs['mc'], outs['mn'], outs['mm'], outs['nk'], outs['nv'],
            outs['gk'], outs['gv'], outs['ss'])
```

```python
import functools

import jax
import jax.numpy as jnp
from jax import lax
from jax.experimental import pallas as pl
from jax.experimental.pallas import tpu as pltpu

D_MODEL = 2048
BATCH = 16
SEQ = 256
DEPTH = 4
DEC_BATCH = 2
DEC_SEQ = 2048
PAST_LEN = 512
GRID_W = 64
N_MIXERS = 4

MLSTM_HEADS = 8
MLSTM_DV = D_MODEL // MLSTM_HEADS
MLSTM_DQK = MLSTM_DV // 2
MLSTM_CHUNK = 64

NAT_HEADS = 16
NAT_HD = D_MODEL // NAT_HEADS
NAT_KH = 8
NAT_KW = 16

GQA_HEADS = 16
GQA_KV_HEADS = 4
GQA_HD = D_MODEL // GQA_HEADS
ROPE_THETA = 10000.0

SSD_D_INNER = 2 * D_MODEL
SSD_HEADDIM = 64
SSD_HEADS = SSD_D_INNER // SSD_HEADDIM
SSD_GROUPS = 8
SSD_STATE = 128
SSD_CONV = 5
SSD_CHUNK = 128
SSD_R = SSD_HEADS // SSD_GROUPS
SSD_GW = SSD_R * SSD_HEADDIM

D_FF = 7 * D_MODEL // 2
N_EXPERTS = 8
TOP_K = 2

EPS = 1e-6
NEG = -1e30

N_PROMPT = BATCH * SEQ
N_LATENT = DEC_BATCH * DEC_SEQ
N_TOK = N_PROMPT + N_LATENT
N_GROUPS = 1 + DEC_BATCH
LANE = 128
VMEM_LIMIT = 56 * 1024 * 1024

F32 = jnp.float32
BF16 = jnp.bfloat16
HIGHEST = lax.Precision.HIGHEST
NT = (((1,), (1,)), ((), ()))
TN = (((0,), (0,)), ((), ()))


def _params(sem, vmem=VMEM_LIMIT):
    return pltpu.CompilerParams(dimension_semantics=sem, vmem_limit_bytes=vmem)


def _group_of_tile(i, tm):
    return jnp.maximum(i * tm // DEC_SEQ - (N_PROMPT // DEC_SEQ - 1), 0)


def _rms(x, g):
    return x * lax.rsqrt(jnp.mean(x * x, axis=-1, keepdims=True) + EPS) * g


def _softplus(x):
    return jnp.maximum(x, 0.0) + jnp.log(1.0 + jnp.exp(-jnp.abs(x)))


def _iota(shape, dim):
    return lax.broadcasted_iota(jnp.int32, shape, dim)


def _mod_kernel(c_ref, w_ref, b_ref, o_ref):
    c = c_ref[...]
    s = (c * jax.nn.sigmoid(c)).astype(BF16)
    o_ref[0] = jnp.dot(s, w_ref[0].astype(BF16), preferred_element_type=F32) + b_ref[0]


def modulation_all(cvecs, w_mod, b_mod):
    tn = 1024
    n = w_mod.shape[-1]
    return pl.pallas_call(
        _mod_kernel,
        grid=(DEPTH, n // tn),
        in_specs=[pl.BlockSpec((8, D_MODEL), lambda l, j: (0, 0)),
                  pl.BlockSpec((1, D_MODEL, tn), lambda l, j: (l, 0, j)),
                  pl.BlockSpec((1, 1, tn), lambda l, j: (l, 0, j))],
        out_specs=pl.BlockSpec((1, 8, tn), lambda l, j: (l, 0, j)),
        out_shape=jax.ShapeDtypeStruct((DEPTH, 8, n), F32),
        compiler_params=_params(("parallel", "parallel")),
        name="modulation",
    )(cvecs, w_mod, b_mod.reshape(DEPTH, 1, n))


def _norm_linear_kernel(x_ref, g_ref, sc_ref, sh_ref, w_ref, o_ref, u_ref):
    @pl.when(pl.program_id(1) == 0)
    def _():
        u = _rms(x_ref[...], g_ref[...]) * (1.0 + sc_ref[0]) + sh_ref[0]
        u_ref[...] = u.astype(BF16)

    o_ref[...] = jnp.dot(u_ref[...], w_ref[...].astype(BF16), preferred_element_type=F32).astype(o_ref.dtype)


def norm_linear(h, g, scale, shift, w, *, n_out=None, col0=0, tm=1024, tn=512, out_dtype=F32, name="norm_linear"):
    m, d = h.shape
    n_out = w.shape[1] if n_out is None else n_out
    return pl.pallas_call(
        _norm_linear_kernel,
        grid=(m // tm, pl.cdiv(n_out, tn)),
        in_specs=[pl.BlockSpec((tm, d), lambda i, j: (i, 0)),
                  pl.BlockSpec((1, d), lambda i, j: (0, 0)),
                  pl.BlockSpec((1, 1, d), lambda i, j: (_group_of_tile(i, tm), 0, 0)),
                  pl.BlockSpec((1, 1, d), lambda i, j: (_group_of_tile(i, tm), 0, 0)),
                  pl.BlockSpec((d, tn), lambda i, j: (0, j + col0))],
        out_specs=pl.BlockSpec((tm, tn), lambda i, j: (i, j)),
        out_shape=jax.ShapeDtypeStruct((m, n_out), out_dtype),
        scratch_shapes=[pltpu.VMEM((tm, d), BF16)],
        compiler_params=_params(("parallel", "arbitrary")),
        name=name,
    )(h, g.reshape(1, d), scale, shift, w)


def _norm_swiglu_kernel(x_ref, g_ref, sc_ref, sh_ref, wg_ref, wu_ref, o_ref, u_ref):
    @pl.when(pl.program_id(1) == 0)
    def _():
        u = _rms(x_ref[...], g_ref[...]) * (1.0 + sc_ref[0]) + sh_ref[0]
        u_ref[...] = u.astype(BF16)

    u = u_ref[...]
    a = jnp.dot(u, wg_ref[...].astype(BF16), preferred_element_type=F32)
    b = jnp.dot(u, wu_ref[...].astype(BF16), preferred_element_type=F32)
    o_ref[...] = (a * jax.nn.sigmoid(a) * b).astype(o_ref.dtype)


def norm_swiglu_in(h, g, scale, shift, w_in, *, tm=1024, tn=256):
    m, d = h.shape
    f = w_in.shape[1] // 2
    nj = f // tn
    return pl.pallas_call(
        _norm_swiglu_kernel,
        grid=(m // tm, nj),
        in_specs=[pl.BlockSpec((tm, d), lambda i, j: (i, 0)),
                  pl.BlockSpec((1, d), lambda i, j: (0, 0)),
                  pl.BlockSpec((1, 1, d), lambda i, j: (_group_of_tile(i, tm), 0, 0)),
                  pl.BlockSpec((1, 1, d), lambda i, j: (_group_of_tile(i, tm), 0, 0)),
                  pl.BlockSpec((d, tn), lambda i, j: (0, j)),
                  pl.BlockSpec((d, tn), lambda i, j: (0, j + nj))],
        out_specs=pl.BlockSpec((tm, tn), lambda i, j: (i, j)),
        out_shape=jax.ShapeDtypeStruct((m, f), BF16),
        scratch_shapes=[pltpu.VMEM((tm, d), BF16)],
        compiler_params=_params(("parallel", "arbitrary")),
        name="norm_swiglu_in",
    )(h, g.reshape(1, d), scale, shift, w_in, w_in)


def _linear_out_kernel(*refs, prescale, kdim):
    if prescale:
        a_ref, w_ref, h_ref, g_ref, gate_ref, ssq_ref, gn_ref, o_ref = refs
    else:
        a_ref, w_ref, h_ref, g_ref, gate_ref, o_ref = refs
    k = pl.program_id(1)
    a = a_ref[...]
    if prescale:
        a = a * lax.rsqrt(ssq_ref[:, :1] * (1.0 / kdim) + EPS) * gn_ref[...]
    part = jnp.dot(a.astype(BF16), w_ref[...].astype(BF16), preferred_element_type=F32)

    @pl.when(k == 0)
    def _():
        o_ref[...] = part

    @pl.when(k > 0)
    def _():
        o_ref[...] += part

    @pl.when(k == pl.num_programs(1) - 1)
    def _():
        o_ref[...] = h_ref[...] + gate_ref[0] * _rms(o_ref[...], g_ref[...])


def linear_out(a, w, h, g, gate, *, ssq=None, gn=None, tm=512, tk=512, name="linear_out"):
    m, kdim = a.shape
    d = w.shape[1]
    prescale = ssq is not None
    in_specs = [pl.BlockSpec((tm, tk), lambda i, k: (i, k)),
                pl.BlockSpec((tk, d), lambda i, k: (k, 0)),
                pl.BlockSpec((tm, d), lambda i, k: (i, 0)),
                pl.BlockSpec((1, d), lambda i, k: (0, 0)),
                pl.BlockSpec((1, 1, d), lambda i, k: (_group_of_tile(i, tm), 0, 0))]
    args = [a, w, h, g.reshape(1, d), gate]
    if prescale:
        in_specs += [pl.BlockSpec((tm, LANE), lambda i, k: (i, 0)),
                     pl.BlockSpec((1, tk), lambda i, k: (0, k))]
        args += [ssq, gn.reshape(1, kdim)]
    return pl.pallas_call(
        functools.partial(_linear_out_kernel, prescale=prescale, kdim=kdim),
        grid=(m // tm, kdim // tk),
        in_specs=in_specs,
        out_specs=pl.BlockSpec((tm, d), lambda i, k: (i, 0)),
        out_shape=jax.ShapeDtypeStruct((m, d), F32),
        compiler_params=_params(("parallel", "arbitrary")),
        name=name,
    )(*args)


def _norm_router_kernel(x_ref, g_ref, sc_ref, sh_ref, wr_ref, br_ref, u_ref, rt_ref):
    u = _rms(x_ref[...], g_ref[...]) * (1.0 + sc_ref[0]) + sh_ref[0]
    u_ref[...] = u
    lg = jnp.dot(u, wr_ref[...], preferred_element_type=F32, precision=HIGHEST) + br_ref[...]
    lane = _iota(lg.shape, 1)
    lanef = lane.astype(F32)
    lg = jnp.where(lane < N_EXPERTS, lg, -jnp.inf)
    m1 = jnp.max(lg, axis=-1, keepdims=True)
    i1 = jnp.min(jnp.where(lg == m1, lanef, float(LANE)), axis=-1, keepdims=True)
    lg2 = jnp.where(lanef == i1, -jnp.inf, lg)
    m2 = jnp.max(lg2, axis=-1, keepdims=True)
    i2 = jnp.min(jnp.where(lg2 == m2, lanef, float(LANE)), axis=-1, keepdims=True)
    e = jnp.exp(m2 - m1)
    w1 = 1.0 / (1.0 + e)
    w2 = e / (1.0 + e)
    rt_ref[...] = jnp.where(lane == 0, i1, jnp.where(lane == 1, i2, jnp.where(lane == 2, w1,
                            jnp.where(lane == 3, w2, 0.0))))


def norm_router(h, g, scale, shift, w_router, b_router, *, tm=512):
    m, d = h.shape
    wr = jnp.zeros((d, LANE), F32).at[:, :N_EXPERTS].set(w_router)
    br = jnp.zeros((1, LANE), F32).at[0, :N_EXPERTS].set(b_router)
    return pl.pallas_call(
        _norm_router_kernel,
        grid=(m // tm,),
        in_specs=[pl.BlockSpec((tm, d), lambda i: (i, 0)),
                  pl.BlockSpec((1, d), lambda i: (0, 0)),
                  pl.BlockSpec((1, 1, d), lambda i: (_group_of_tile(i, tm), 0, 0)),
                  pl.BlockSpec((1, 1, d), lambda i: (_group_of_tile(i, tm), 0, 0)),
                  pl.BlockSpec((d, LANE), lambda i: (0, 0)),
                  pl.BlockSpec((1, LANE), lambda i: (0, 0))],
        out_specs=[pl.BlockSpec((tm, d), lambda i: (i, 0)),
                   pl.BlockSpec((tm, LANE), lambda i: (i, 0))],
        out_shape=[jax.ShapeDtypeStruct((m, d), F32), jax.ShapeDtypeStruct((m, LANE), F32)],
        compiler_params=_params(("parallel",)),
        name="norm_router",
    )(h, g.reshape(1, d), scale, shift, wr, br)


MOE_TM = 512


def _row_copy(src_hbm, idx, dst, r, sem):
    return pltpu.make_async_copy(src_hbm.at[pl.ds(idx, 1), :], dst.at[pl.ds(r, 1), :], sem)


def _gather_all(idx_ref, base, src_hbm, dst, sem, n):
    def issue(r, c):
        _row_copy(src_hbm, idx_ref[base + r], dst, r, sem).start()
        return c

    lax.fori_loop(0, n, issue, 0)

    def drain(r, c):
        _row_copy(src_hbm, 0, dst, r, sem).wait()
        return c

    lax.fori_loop(0, n, drain, 0)


def _moe_gather_kernel(tok_ref, nu_ref, u_hbm, o_ref, buf, sem):
    @pl.when(pl.program_id(0) < nu_ref[0])
    def _():
        _gather_all(tok_ref, pl.program_id(0) * MOE_TM, u_hbm, buf, sem, MOE_TM)
        o_ref[...] = buf[...].astype(o_ref.dtype)

    @pl.when(pl.program_id(0) >= nu_ref[0])
    def _():
        o_ref[...] = jnp.zeros_like(o_ref)


def moe_gather(u, row_tok, n_used):
    r = row_tok.shape[0]
    d = u.shape[1]
    return pl.pallas_call(
        _moe_gather_kernel,
        grid_spec=pltpu.PrefetchScalarGridSpec(
            num_scalar_prefetch=2,
            grid=(r // MOE_TM,),
            in_specs=[pl.BlockSpec(memory_space=pl.ANY)],
            out_specs=pl.BlockSpec((MOE_TM, d), lambda b, tok, nu: (b, 0)),
            scratch_shapes=[pltpu.VMEM((MOE_TM, d), F32), pltpu.SemaphoreType.DMA(())],
        ),
        out_shape=jax.ShapeDtypeStruct((r, d), BF16),
        compiler_params=_params(("arbitrary",)),
        name="moe_gather",
    )(row_tok, n_used, u)


def _moe_in_kernel(be_ref, nu_ref, x_ref, wg_ref, wu_ref, o_ref):
    @pl.when(pl.program_id(1) < nu_ref[0])
    def _():
        x = x_ref[...]
        a = jnp.dot(x, wg_ref[0].astype(BF16), preferred_element_type=F32)
        b = jnp.dot(x, wu_ref[0].astype(BF16), preferred_element_type=F32)
        o_ref[...] = (a * jax.nn.sigmoid(a) * b).astype(o_ref.dtype)

    @pl.when(pl.program_id(1) >= nu_ref[0])
    def _():
        o_ref[...] = jnp.zeros_like(o_ref)


def _used_block(b, nu):
    return jnp.minimum(b, nu[0] - 1)


def moe_in(x_rows, blk_expert, n_used, w_in, *, tn=512):
    r, d = x_rows.shape
    f = w_in.shape[2] // 2
    nj = f // tn
    nb = r // MOE_TM
    return pl.pallas_call(
        _moe_in_kernel,
        grid_spec=pltpu.PrefetchScalarGridSpec(
            num_scalar_prefetch=2,
            grid=(nj, nb),
            in_specs=[pl.BlockSpec((MOE_TM, d), lambda j, b, be, nu: (_used_block(b, nu), 0)),
                      pl.BlockSpec((1, d, tn), lambda j, b, be, nu: (be[_used_block(b, nu)], 0, j)),
                      pl.BlockSpec((1, d, tn), lambda j, b, be, nu: (be[_used_block(b, nu)], 0, j + nj))],
            out_specs=pl.BlockSpec((MOE_TM, tn), lambda j, b, be, nu: (b, j)),
        ),
        out_shape=jax.ShapeDtypeStruct((r, f), BF16),
        compiler_params=_params(("arbitrary", "arbitrary")),
        name="moe_in",
    )(blk_expert, n_used, x_rows, w_in, w_in)


def _moe_out_kernel(be_ref, nu_ref, a_ref, w_ref, rw_ref, o_ref):
    @pl.when(pl.program_id(1) < nu_ref[0])
    def _():
        y = jnp.dot(a_ref[...], w_ref[0].astype(BF16), preferred_element_type=F32)
        o_ref[...] = y * rw_ref[...]

    @pl.when(pl.program_id(1) >= nu_ref[0])
    def _():
        o_ref[...] = jnp.zeros_like(o_ref)


def moe_out(act_rows, blk_expert, n_used, w_out, row_w, *, tn=256):
    r, f = act_rows.shape
    d = w_out.shape[2]
    nb = r // MOE_TM
    return pl.pallas_call(
        _moe_out_kernel,
        grid_spec=pltpu.PrefetchScalarGridSpec(
            num_scalar_prefetch=2,
            grid=(d // tn, nb),
            in_specs=[pl.BlockSpec((MOE_TM, f), lambda j, b, be, nu: (_used_block(b, nu), 0)),
                      pl.BlockSpec((1, f, tn), lambda j, b, be, nu: (be[_used_block(b, nu)], 0, j)),
                      pl.BlockSpec((MOE_TM, 1), lambda j, b, be, nu: (_used_block(b, nu), 0))],
            out_specs=pl.BlockSpec((MOE_TM, tn), lambda j, b, be, nu: (b, j)),
        ),
        out_shape=jax.ShapeDtypeStruct((r, d), F32),
        compiler_params=_params(("arbitrary", "arbitrary")),
        name="moe_out",
    )(blk_expert, n_used, act_rows, w_out, row_w.reshape(r, 1))


COMBINE_TM = 256


def _moe_combine_kernel(pos_ref, y_hbm, h_ref, g_ref, gate_ref, o_ref, buf, sem):
    n = TOP_K * COMBINE_TM
    _gather_all(pos_ref, pl.program_id(0) * n, y_hbm, buf, sem, n)
    f = buf[:COMBINE_TM, :] + buf[COMBINE_TM:, :]
    o_ref[...] = h_ref[...] + gate_ref[0] * _rms(f, g_ref[...])


def moe_combine(y_rows, pos_blocked, h, g, gate):
    m, d = h.shape
    tm = COMBINE_TM
    return pl.pallas_call(
        _moe_combine_kernel,
        grid_spec=pltpu.PrefetchScalarGridSpec(
            num_scalar_prefetch=1,
            grid=(m // tm,),
            in_specs=[pl.BlockSpec(memory_space=pl.ANY),
                      pl.BlockSpec((tm, d), lambda i, pos: (i, 0)),
                      pl.BlockSpec((1, d), lambda i, pos: (0, 0)),
                      pl.BlockSpec((1, 1, d), lambda i, pos: (_group_of_tile(i, tm), 0, 0))],
            out_specs=pl.BlockSpec((tm, d), lambda i, pos: (i, 0)),
            scratch_shapes=[pltpu.VMEM((TOP_K * tm, d), F32), pltpu.SemaphoreType.DMA(())],
        ),
        out_shape=jax.ShapeDtypeStruct((m, d), F32),
        compiler_params=_params(("arbitrary",)),
        name="moe_combine",
    )(pos_blocked, y_rows, h, g.reshape(1, d), gate)


def moe_layer(h, g, scale, shift, w_router, b_router, w_in, w_out, g_post, gate):
    t = h.shape[0]
    u, route = norm_router(h, g, scale, shift, w_router, b_router)
    expert = route[:, :TOP_K].astype(jnp.int32).reshape(-1)
    weight = route[:, TOP_K:2 * TOP_K].reshape(-1)
    n_assign = t * TOP_K
    onehot = (expert[:, None] == jnp.arange(N_EXPERTS, dtype=jnp.int32)[None, :]).astype(jnp.int32)
    rank = jnp.sum((jnp.cumsum(onehot, axis=0) - onehot) * onehot, axis=1)
    counts = jnp.sum(onehot, axis=0)
    padded = (counts + MOE_TM - 1) // MOE_TM * MOE_TM
    pad_end = jnp.cumsum(padded)
    pad_start = pad_end - padded
    pos = (pad_start[expert] + rank).astype(jnp.int32)
    n_blocks = n_assign // MOE_TM + N_EXPERTS
    n_rows = n_blocks * MOE_TM
    token = jnp.repeat(jnp.arange(t, dtype=jnp.int32), TOP_K)
    row_tok = jnp.zeros((n_rows,), jnp.int32).at[pos].set(token)
    row_w = jnp.zeros((n_rows,), F32).at[pos].set(weight)
    blk_expert = jnp.minimum(
        jnp.searchsorted(pad_end, jnp.arange(n_blocks, dtype=jnp.int32) * MOE_TM, side='right'),
        N_EXPERTS - 1).astype(jnp.int32)
    n_used = (pad_end[-1] // MOE_TM).astype(jnp.int32).reshape(1)
    pos_blocked = pos.reshape(t // COMBINE_TM, COMBINE_TM, TOP_K).transpose(0, 2, 1).reshape(-1)

    x_rows = moe_gather(u, row_tok, n_used)
    act = moe_in(x_rows, blk_expert, n_used, w_in)
    y_rows = moe_out(act, blk_expert, n_used, w_out, row_w)
    return moe_combine(y_rows, pos_blocked, h, g_post, gate)


def _attn_kernel(*refs, scale, two, aliased):
    if aliased:
        refs = refs[1:]
    if two:
        q_ref, k_ref, v_ref, kc_ref, vc_ref, o_ref = refs
    else:
        q_ref, k_ref, v_ref, o_ref = refs
    q = q_ref[...].astype(BF16)
    s = lax.dot_general(q, k_ref[...].astype(BF16), NT, preferred_element_type=F32) * scale
    m = jnp.max(s, axis=-1, keepdims=True)
    if two:
        s2 = lax.dot_general(q, kc_ref[0].astype(BF16), NT, preferred_element_type=F32) * scale
        m = jnp.maximum(m, jnp.max(s2, axis=-1, keepdims=True))
    p = jnp.exp(s - m)
    l = jnp.sum(p, axis=-1, keepdims=True)
    o = jnp.dot(p.astype(BF16), v_ref[...].astype(BF16), preferred_element_type=F32)
    if two:
        p2 = jnp.exp(s2 - m)
        l = l + jnp.sum(p2, axis=-1, keepdims=True)
        o = o + jnp.dot(p2.astype(BF16), vc_ref[0].astype(BF16), preferred_element_type=F32)
    o_ref[...] = (o / l).astype(o_ref.dtype)


def attention(q_arr, k_arr, v_arr, *, n_batch, n_heads, nq, nk, tq, row0, q_col, k_col, v_col,
              k_cache=None, v_cache=None, kv_of_head=lambda h: h, into=None, name="attention"):
    hd = LANE
    nqb = nq // tq
    two = k_cache is not None
    aliased = into is not None
    in_specs = [pl.BlockSpec((tq, hd), lambda b, h, i: (row0 // tq + b * nqb + i, q_col(h))),
                pl.BlockSpec((nk, hd), lambda b, h, i: (row0 // nk + b, k_col(h))),
                pl.BlockSpec((nk, hd), lambda b, h, i: (row0 // nk + b, v_col(h)))]
    args = [q_arr, k_arr, v_arr]
    if two:
        n_past = k_cache.shape[1]
        in_specs += [pl.BlockSpec((1, n_past, hd), lambda b, h, i: (b, 0, kv_of_head(h))),
                     pl.BlockSpec((1, n_past, hd), lambda b, h, i: (b, 0, kv_of_head(h)))]
        args += [k_cache, v_cache]
    if aliased:
        in_specs = [pl.BlockSpec(memory_space=pl.ANY)] + in_specs
        args = [into] + args
    return pl.pallas_call(
        functools.partial(_attn_kernel, scale=hd ** -0.5, two=two, aliased=aliased),
        grid=(n_batch, n_heads, nqb),
        in_specs=in_specs,
        out_specs=pl.BlockSpec((tq, hd), lambda b, h, i: (row0 // tq + b * nqb + i, h)),
        out_shape=jax.ShapeDtypeStruct((N_TOK, n_heads * hd), BF16),
        input_output_aliases={0: 0} if aliased else {},
        compiler_params=_params(("parallel", "parallel", "arbitrary")),
        name=name,
    )(*args)


def _nat_latent_kernel(into_ref, q_ref, k_ref, v_ref, kc_ref, vc_ref, b_ref, o_ref, kb_ref, vb_ref, *, rows):
    del into_ref
    w, kh = GRID_W, NAT_KH
    nwin = kh * w
    half = kh // 2
    scale = NAT_HD ** -0.5
    qi = _iota((w, nwin), 0)
    kk = _iota((w, nwin), 1) % w
    c_start = jnp.clip(qi - NAT_KW // 2, 0, w - NAT_KW)
    col_ok = (kk >= c_start) & (kk < c_start + NAT_KW)
    kb_ref[...] = k_ref[...].astype(BF16)
    vb_ref[...] = v_ref[...].astype(BF16)
    kc = kc_ref[0].astype(BF16)
    vc = vc_ref[0].astype(BF16)

    def one_row(r, start, rel):
        q0 = r * w
        k0 = start * w
        if not isinstance(r, int):
            q0 = pl.multiple_of(q0, w)
            k0 = pl.multiple_of(k0, w)
        q = q_ref[pl.ds(q0, w), :].astype(BF16)
        s1 = lax.dot_general(q, kb_ref[pl.ds(k0, nwin), :], NT, preferred_element_type=F32) * scale + b_ref[0, rel]
        s1 = jnp.where(col_ok, s1, NEG)
        s2 = lax.dot_general(q, kc, NT, preferred_element_type=F32) * scale
        m = jnp.maximum(jnp.max(s1, axis=-1, keepdims=True), jnp.max(s2, axis=-1, keepdims=True))
        p1 = jnp.exp(s1 - m)
        p2 = jnp.exp(s2 - m)
        l = jnp.sum(p1, axis=-1, keepdims=True) + jnp.sum(p2, axis=-1, keepdims=True)
        o = (jnp.dot(p1.astype(BF16), vb_ref[pl.ds(k0, nwin), :], preferred_element_type=F32)
             + jnp.dot(p2.astype(BF16), vc, preferred_element_type=F32))
        o_ref[pl.ds(q0, w), :] = (o / l).astype(o_ref.dtype)

    last_start = rows - kh
    for r in range(half):
        one_row(r, 0, kh - 1 - r)

    def interior(r, c):
        one_row(r, r - half, kh - 1 - half)
        return c

    lax.fori_loop(half, last_start + half + 1, interior, 0)
    for r in range(last_start + half + 1, rows):
        one_row(r, last_start, last_start - r + kh - 1)


def nat_bias_slabs(rpb):
    c = jnp.arange(GRID_W)
    col_off = jnp.clip(c[None, :] - c[:, None], -(NAT_KW - 1), NAT_KW - 1) + NAT_KW - 1
    t = rpb[:, :, col_off]
    rel = jnp.arange(NAT_KH)[:, None] + jnp.arange(NAT_KH)[None, :]
    return t[:, rel].transpose(0, 1, 3, 2, 4).reshape(rpb.shape[0], NAT_KH, GRID_W, NAT_KH * GRID_W)


def nat_latent(qkv, k_cache, v_cache, bias, into):
    nh, hd = NAT_HEADS, NAT_HD
    rows = DEC_SEQ // GRID_W
    rb0 = N_PROMPT // DEC_SEQ
    blk = lambda c0: pl.BlockSpec((DEC_SEQ, hd), lambda b, h: (rb0 + b, c0 + h))
    return pl.pallas_call(
        functools.partial(_nat_latent_kernel, rows=rows),
        grid=(DEC_BATCH, nh),
        in_specs=[pl.BlockSpec(memory_space=pl.ANY), blk(0), blk(nh), blk(2 * nh),
                  pl.BlockSpec((1, PAST_LEN, hd), lambda b, h: (b, 0, h)),
                  pl.BlockSpec((1, PAST_LEN, hd), lambda b, h: (b, 0, h)),
                  pl.BlockSpec((1, NAT_KH, GRID_W, NAT_KH * GRID_W), lambda b, h: (h, 0, 0, 0))],
        out_specs=pl.BlockSpec((DEC_SEQ, hd), lambda b, h: (rb0 + b, h)),
        out_shape=jax.ShapeDtypeStruct((N_TOK, nh * hd), BF16),
        input_output_aliases={0: 0},
        scratch_shapes=[pltpu.VMEM((DEC_SEQ, hd), BF16), pltpu.VMEM((DEC_SEQ, hd), BF16)],
        compiler_params=_params(("parallel", "parallel")),
        name="nat_latent",
    )(into, qkv, qkv, qkv, k_cache, v_cache, bias)


def _qk_norm_rope_kernel(x_ref, g_ref, cos_ref, sa_ref, sb_ref, o_ref, *, n_heads):
    cos, sa, sb = cos_ref[...], sa_ref[...], sb_ref[...]
    quarter = GQA_HD // 4
    for hh in range(n_heads):
        sl = slice(hh * GQA_HD, (hh + 1) * GQA_HD)
        y = _rms(x_ref[:, sl], g_ref[:, sl])
        up = pltpu.roll(y, GQA_HD - quarter, axis=1)
        dn = pltpu.roll(y, quarter, axis=1)
        o_ref[:, sl] = y * cos + up * sa + dn * sb


def qk_norm_rope(qkv, gains, cos, sa, sb, *, n_heads, tm=512):
    m = qkv.shape[0]
    w = n_heads * GQA_HD
    return pl.pallas_call(
        functools.partial(_qk_norm_rope_kernel, n_heads=n_heads),
        grid=(m // tm,),
        in_specs=[pl.BlockSpec((tm, w), lambda i: (i, 0)),
                  pl.BlockSpec((1, w), lambda i: (0, 0)),
                  pl.BlockSpec((tm, GQA_HD), lambda i: (i, 0)),
                  pl.BlockSpec((tm, GQA_HD), lambda i: (i, 0)),
                  pl.BlockSpec((tm, GQA_HD), lambda i: (i, 0))],
        out_specs=pl.BlockSpec((tm, w), lambda i: (i, 0)),
        out_shape=jax.ShapeDtypeStruct((m, w), F32),
        compiler_params=_params(("parallel",)),
        name="qk_norm_rope",
    )(qkv, gains, cos, sa, sb)


def rope_tables():
    t = jnp.arange(DEC_SEQ)
    row = (t // GRID_W).astype(F32)
    col = (t % GRID_W).astype(F32)
    half = GQA_HD // 2
    freqs = ROPE_THETA ** (-jnp.arange(0, half, 2, dtype=F32) / half)
    ar = row[:, None] * freqs
    ac = col[:, None] * freqs
    z = jnp.zeros_like(ar)
    cos = jnp.concatenate([jnp.cos(ar), jnp.cos(ar), jnp.cos(ac), jnp.cos(ac)], axis=-1)
    sa = jnp.concatenate([-jnp.sin(ar), z, -jnp.sin(ac), z], axis=-1)
    sb = jnp.concatenate([z, jnp.sin(ar), z, jnp.sin(ac)], axis=-1)
    ones = jnp.ones((N_PROMPT, GQA_HD), F32)
    zeros = jnp.zeros((N_PROMPT, GQA_HD), F32)
    tile = lambda a: jnp.tile(a, (DEC_BATCH, 1))
    return (jnp.concatenate([ones, tile(cos)], 0), jnp.concatenate([zeros, tile(sa)], 0),
            jnp.concatenate([zeros, tile(sb)], 0))


def _mlstm_kernel(*refs, seq, zero_state):
    if zero_state:
        (q_ref, k_ref, v_ref, og_ref, gt_ref, bg_ref, gh_ref,
         a_ref, cf_ref, nf_ref, mf_ref, hf_ref, hb_ref, cs_ref, ns_ref, m_ref) = refs
    else:
        (into_ref, q_ref, k_ref, v_ref, og_ref, gt_ref, bg_ref, gh_ref, c0_ref, n0_ref, m0_ref,
         a_ref, hf_ref, hb_ref, cs_ref, ns_ref, m_ref) = refs
        del into_ref
    L = MLSTM_CHUNK
    nc = seq // L
    head = pl.program_id(1)
    scale = MLSTM_DQK ** -0.5

    if zero_state:
        cs_ref[...] = jnp.zeros_like(cs_ref)
        ns_ref[...] = jnp.zeros_like(ns_ref)
        m_ref[...] = jnp.full_like(m_ref, NEG)
    else:
        cs_ref[...] = c0_ref[0, :, 0]
        ns_ref[...] = n0_ref[0, :, 0]
        m_ref[...] = m0_ref[0, :, 0]

    rr = _iota((LANE, LANE), 0)
    cc = _iota((LANE, LANE), 1)
    sel = jnp.where((cc < 4) & (rr == MLSTM_HEADS * cc + head), 1.0, 0.0).astype(F32)
    lane = _iota((L, LANE), 1)
    jj = _iota((L, L), 0)
    ss = _iota((L, L), 1)
    lower = (ss <= jj)
    upper = (ss >= jj)
    lowerf = lower.astype(F32)

    def chunk_gates(c0):
        g = gt_ref[pl.ds(c0, L), :] + bg_ref[...]
        x = jnp.dot(g, sel, preferred_element_type=F32, precision=HIGHEST)
        logsig = jnp.minimum(x, 0.0) - jnp.log(1.0 + jnp.exp(-jnp.abs(x)))
        x = jnp.where((lane == 1) | (lane == 3), logsig, x)
        pre = jnp.dot(lowerf, x, preferred_element_type=F32, precision=HIGHEST)
        tot = pre[L - 1:L, :]
        suf = tot - pre + x
        y = jnp.where(lane == 1, pre, jnp.where(lane == 3, suf, x))
        return y, y.T, tot

    def direction(d, c0):
        y, yt, tot = chunk_gates(c0)
        li_col, b_col = y[:, 2 * d:2 * d + 1], y[:, 2 * d + 1:2 * d + 2]
        li_row, b_row = yt[2 * d:2 * d + 1, :], yt[2 * d + 1:2 * d + 2, :]
        bl = tot[:, 2 * d + 1:2 * d + 2]
        m = m_ref[d][:, :1]
        dlog = jnp.where(lower if d == 0 else upper, b_col - b_row + li_row, -jnp.inf)
        inter = b_col + m
        mj = jnp.maximum(inter, jnp.max(dlog, axis=-1, keepdims=True))
        dw = jnp.exp(dlog - mj)
        iw = jnp.exp(inter - mj)
        qc = q_ref[pl.ds(c0, L), :] * scale
        kc = k_ref[pl.ds(c0, L), :]
        qb = qc.astype(BF16)
        vb = v_ref[pl.ds(c0, L), :].astype(BF16)
        sc = lax.dot_general(qb, kc.astype(BF16), NT, preferred_element_type=F32) * dw
        cs = cs_ref[d]
        ns = ns_ref[d]
        num = (jnp.dot(sc.astype(BF16), vb, preferred_element_type=F32)
               + iw * jnp.dot(qb, cs.astype(BF16), preferred_element_type=F32))
        den = jnp.sum(sc, axis=-1, keepdims=True) + iw * jnp.sum(qc * ns, axis=-1, keepdims=True)
        hc = num / jnp.maximum(jnp.abs(den), jnp.exp(-mj))
        elog = bl - b_col + li_col
        carry_log = bl + m
        m_new = jnp.maximum(carry_log, jnp.max(elog, axis=0, keepdims=True))
        ew = jnp.exp(elog - m_new)
        cw = jnp.exp(carry_log - m_new)
        kw = ew * kc
        cs_ref[d] = cw * cs + lax.dot_general(kw.astype(BF16), vb, TN, preferred_element_type=F32)
        ns_ref[d] = cw * ns + jnp.sum(kw, axis=0, keepdims=True)
        m_ref[d] = jnp.broadcast_to(m_new, (1, LANE))
        return hc

    def body(c, carry):
        cf = pl.multiple_of(c * L, L)
        cb = pl.multiple_of((nc - 1 - c) * L, L)
        hf_ref[pl.ds(cf, L), :] = direction(0, cf)
        hb_ref[pl.ds(cb, L), :] = direction(1, cb)
        return carry

    lax.fori_loop(0, nc, body, 0)
    hsum = _rms(hf_ref[...] + hb_ref[...], gh_ref[...])
    a_ref[...] = (hsum * jax.nn.sigmoid(og_ref[...])).astype(a_ref.dtype)
    if zero_state:
        cf_ref[0, :, 0] = cs_ref[...]
        nf_ref[0, :, 0] = ns_ref[...]
        mf_ref[0, :, 0] = m_ref[...]


def mlstm_scan(proj, gates, b_gate, g_head, *, n_batch, seq, row0, state=None, into=None):
    nh, dqk, dv = MLSTM_HEADS, MLSTM_DQK, MLSTM_DV
    rb0 = row0 // seq
    zero_state = state is None
    in_specs = [pl.BlockSpec((seq, dqk), lambda b, h: (rb0 + b, h)),
                pl.BlockSpec((seq, dqk), lambda b, h: (rb0 + b, nh + h)),
                pl.BlockSpec((seq, dv), lambda b, h: (rb0 + b, nh + h)),
                pl.BlockSpec((seq, dv), lambda b, h: (rb0 + b, 2 * nh + h)),
                pl.BlockSpec((seq, LANE), lambda b, h: (rb0 + b, 0)),
                pl.BlockSpec((1, LANE), lambda b, h: (0, 0)),
                pl.BlockSpec((1, dv), lambda b, h: (0, h))]
    bg = jnp.zeros((1, LANE), F32).at[0, :4 * nh].set(b_gate)
    args = [proj, proj, proj, proj, gates, bg, g_head.reshape(1, nh * dv)]
    out_specs = [pl.BlockSpec((seq, dv), lambda b, h: (rb0 + b, h))]
    out_shape = [jax.ShapeDtypeStruct((N_TOK, nh * dv), BF16)]
    c_spec = pl.BlockSpec((1, 2, 1, dqk, dv), lambda b, h: (b, 0, h, 0, 0))
    n_spec = pl.BlockSpec((1, 2, 1, 1, dqk), lambda b, h: (b, 0, h, 0, 0))
    aliases = {}
    if zero_state:
        out_specs += [c_spec, n_spec, n_spec]
        out_shape += [jax.ShapeDtypeStruct((n_batch, 2, nh, dqk, dv), F32),
                      jax.ShapeDtypeStruct((n_batch, 2, nh, 1, dqk), F32),
                      jax.ShapeDtypeStruct((n_batch, 2, nh, 1, LANE), F32)]
    else:
        c0, n0, m0 = state
        in_specs = [pl.BlockSpec(memory_space=pl.ANY)] + in_specs + [c_spec, n_spec, n_spec]
        args = [into] + args + [c0, n0.reshape(n_batch, 2, nh, 1, dqk),
                                jnp.broadcast_to(m0[..., None, None], (n_batch, 2, nh, 1, LANE))]
        aliases = {0: 0}
    return pl.pallas_call(
        functools.partial(_mlstm_kernel, seq=seq, zero_state=zero_state),
        grid=(n_batch, nh),
        in_specs=in_specs,
        out_specs=out_specs,
        out_shape=out_shape,
        input_output_aliases=aliases,
        scratch_shapes=[pltpu.VMEM((seq, dv), F32), pltpu.VMEM((seq, dv), F32),
                        pltpu.VMEM((2, dqk, dv), F32), pltpu.VMEM((2, 1, dqk), F32), pltpu.VMEM((2, 1, LANE), F32)],
        compiler_params=_params(("parallel", "parallel")),
        name="mlstm_scan_ctx" if zero_state else "mlstm_scan_lat",
    )(*args)


def _conv_silu_kernel(*refs, seq, aliased):
    if aliased:
        refs = refs[1:]
    x_ref, w_ref, b_ref, o_ref = refs
    x = x_ref[...]
    t = _iota((seq, 1), 0)
    half = SSD_CONV // 2
    acc = b_ref[...] + w_ref[half:half + 1, :] * x
    for k in range(SSD_CONV):
        off = k - half
        if off == 0:
            continue
        shifted = pltpu.roll(x, (-off) % seq, axis=0)
        valid = (t + off >= 0) & (t + off < seq)
        acc = acc + w_ref[k:k + 1, :] * jnp.where(valid, shifted, 0.0)
    o_ref[...] = acc * jax.nn.sigmoid(acc)


def conv_silu(proj, conv_w, conv_b, *, n_batch, seq, row0, col0, into=None, tc=512):
    ch = conv_w.shape[1]
    rb0 = row0 // seq
    aliased = into is not None
    in_specs = [pl.BlockSpec((seq, tc), lambda b, j: (rb0 + b, col0 // tc + j)),
                pl.BlockSpec((SSD_CONV, tc), lambda b, j: (0, j)),
                pl.BlockSpec((1, tc), lambda b, j: (0, j))]
    args = [proj, conv_w, conv_b.reshape(1, ch)]
    if aliased:
        in_specs = [pl.BlockSpec(memory_space=pl.ANY)] + in_specs
        args = [into] + args
    return pl.pallas_call(
        functools.partial(_conv_silu_kernel, seq=seq, aliased=aliased),
        grid=(n_batch, ch // tc),
        in_specs=in_specs,
        out_specs=pl.BlockSpec((seq, tc), lambda b, j: (rb0 + b, j)),
        out_shape=jax.ShapeDtypeStruct((N_TOK, ch), F32),
        input_output_aliases={0: 0} if aliased else {},
        compiler_params=_params(("parallel", "parallel")),
        name="ssd_conv",
    )(*args)


def _ssd_kernel(*refs, seq, zero_state):
    if zero_state:
        (xs_ref, bm_ref, cm_ref, dt_ref, z_ref, par_ref, dsk_ref,
         y_ref, ssq_ref, sf_ref, yb_ref, st_ref) = refs
    else:
        (into_y, into_q, xs_ref, bm_ref, cm_ref, dt_ref, z_ref, par_ref, dsk_ref, s0_ref,
         y_ref, ssq_ref, yb_ref, st_ref) = refs
        del into_y, into_q
    L = SSD_CHUNK
    nc = seq // L
    R, P = SSD_R, SSD_HEADDIM
    grp = pl.program_id(1)

    if zero_state:
        st_ref[...] = jnp.zeros_like(st_ref)
    else:
        st_ref[...] = s0_ref[0]

    rr = _iota((LANE, LANE), 0)
    cc = _iota((LANE, LANE), 1)
    sel = jnp.where((cc < 2 * R) & (rr == (cc // R) * SSD_HEADS + grp * R + cc % R), 1.0, 0.0).astype(F32)
    lane = _iota((L, LANE), 1)
    ll = _iota((L, L), 0)
    ss = _iota((L, L), 1)
    lower = (ss <= ll)
    upper = (ss >= ll)
    lowerf = lower.astype(F32)
    dt_bias = par_ref[0, 0:1, :]
    a_neg = -jnp.exp(par_ref[0, 1:2, :])

    def direction(d, c0, out_ref):
        dtc = _softplus(jnp.dot(dt_ref[pl.ds(c0, L), :], sel, preferred_element_type=F32, precision=HIGHEST)
                        + dt_bias)
        dac = dtc * a_neg
        pre = jnp.dot(lowerf, dac, preferred_element_type=F32, precision=HIGHEST)
        tot = pre[L - 1:L, :]
        acum = pre if d == 0 else tot - pre + dac
        acum_t = acum.T
        bmc = bm_ref[pl.ds(c0, L), :].astype(BF16)
        cmc = cm_ref[pl.ds(c0, L), :].astype(BF16)
        cb = lax.dot_general(cmc, bmc, NT, preferred_element_type=F32)
        mask = lower if d == 0 else upper
        for r in range(R):
            u = d * R + r
            col = acum[:, u:u + 1]
            row = acum_t[u:u + 1, :]
            lmat = jnp.exp(jnp.where(mask, col - row, -jnp.inf))
            last = tot[:, u:u + 1]
            xr = xs_ref[pl.ds(c0, L), r * P:(r + 1) * P] * dtc[:, u:u + 1]
            y_diag = jnp.dot((cb * lmat).astype(BF16), xr.astype(BF16), preferred_element_type=F32)
            start = st_ref[d, r]
            y_off = lax.dot_general(cmc, start.astype(BF16), NT, preferred_element_type=F32) * jnp.exp(col)
            decay = jnp.exp(last - col)
            new = lax.dot_general((xr * decay).astype(BF16), bmc, TN, preferred_element_type=F32)
            st_ref[d, r] = jnp.exp(last) * start + new
            out_ref[pl.ds(c0, L), r * P:(r + 1) * P] = y_diag + y_off

    def body(c, carry):
        cf = pl.multiple_of(c * L, L)
        cbk = pl.multiple_of((nc - 1 - c) * L, L)
        direction(0, cf, y_ref)
        direction(1, cbk, yb_ref)
        return carry

    lax.fori_loop(0, nc, body, 0)
    z = z_ref[...]
    y = (y_ref[...] + yb_ref[...] + dsk_ref[...] * xs_ref[...]) * (z * jax.nn.sigmoid(z))
    y_ref[...] = y
    part = jnp.broadcast_to(jnp.sum(y * y, axis=-1, keepdims=True), ssq_ref.shape)

    @pl.when(grp == 0)
    def _():
        ssq_ref[...] = part

    @pl.when(grp > 0)
    def _():
        ssq_ref[...] += part

    if zero_state:
        sf_ref[0] = st_ref[...]


def ssd_scan(proj, xbc, dt_bias, a_log, d_skip, *, n_batch, seq, row0, state=None, into=None):
    di, gw, ng, r = SSD_D_INNER, SSD_GW, SSD_GROUPS, SSD_R
    rb0 = row0 // seq
    zero_state = state is None
    dt_blk = (2 * di + 2 * ng * SSD_STATE) // LANE
    pack = lambda p: p.reshape(2, ng, r).transpose(1, 0, 2).reshape(ng, 2 * r)
    par = jnp.zeros((ng, 8, LANE), F32).at[:, 0, :2 * r].set(pack(dt_bias)).at[:, 1, :2 * r].set(pack(a_log))
    dsk = jnp.repeat(d_skip, SSD_HEADDIM).reshape(1, di)
    in_specs = [pl.BlockSpec((seq, gw), lambda b, g: (rb0 + b, g)),
                pl.BlockSpec((seq, SSD_STATE), lambda b, g: (rb0 + b, di // SSD_STATE + g)),
                pl.BlockSpec((seq, SSD_STATE), lambda b, g: (rb0 + b, di // SSD_STATE + ng + g)),
                pl.BlockSpec((seq, LANE), lambda b, g: (rb0 + b, dt_blk)),
                pl.BlockSpec((seq, gw), lambda b, g: (rb0 + b, g)),
                pl.BlockSpec((1, 8, LANE), lambda b, g: (g, 0, 0)),
                pl.BlockSpec((1, gw), lambda b, g: (0, g))]
    args = [xbc, xbc, xbc, proj, proj, par, dsk]
    out_specs = [pl.BlockSpec((seq, gw), lambda b, g: (rb0 + b, g)),
                 pl.BlockSpec((seq, LANE), lambda b, g: (rb0 + b, 0))]
    out_shape = [jax.ShapeDtypeStruct((N_TOK, di), F32), jax.ShapeDtypeStruct((N_TOK, LANE), F32)]
    s_spec = pl.BlockSpec((1, 2, r, SSD_HEADDIM, SSD_STATE), lambda b, g: (b, 0, g, 0, 0))
    aliases = {}
    if zero_state:
        out_specs.append(s_spec)
        out_shape.append(jax.ShapeDtypeStruct((n_batch, 2, SSD_HEADS, SSD_HEADDIM, SSD_STATE), F32))
    else:
        in_specs = [pl.BlockSpec(memory_space=pl.ANY)] * 2 + in_specs + [s_spec]
        args = list(into) + args + [state]
        aliases = {0: 0, 1: 1}
    return pl.pallas_call(
        functools.partial(_ssd_kernel, seq=seq, zero_state=zero_state),
        grid=(n_batch, ng),
        in_specs=in_specs,
        out_specs=out_specs,
        out_shape=out_shape,
        input_output_aliases=aliases,
        scratch_shapes=[pltpu.VMEM((seq, gw), F32), pltpu.VMEM((2, r, SSD_HEADDIM, SSD_STATE), F32)],
        compiler_params=_params(("parallel", "arbitrary")),
        name="ssd_scan_ctx" if zero_state else "ssd_scan_lat",
    )(*args)


def kernel(x_prompt, x_sample, state_mlstm_c, state_mlstm_n, state_mlstm_m, cache_nat_k, cache_nat_v, cache_gqa_k, cache_gqa_v, state_ssd, c, c_ctx, w_mod, b_mod, norm_g, mlstm_w_in, mlstm_b_gate, mlstm_g_head, mlstm_w_out, nat_w_qkv, nat_rpb, nat_w_out, gqa_w_qkv, gqa_q_g, gqa_k_g, gqa_w_out, ssd_w_in, ssd_conv_w, ssd_conv_b, ssd_dt_bias, ssd_a_log, ssd_d, ssd_g_norm, ssd_w_out, ffn_w_in, ffn_w_out, moe_w_router, moe_b_router, moe_w_in, moe_w_out):
    d = D_MODEL
    h = jnp.concatenate([x_prompt.reshape(N_PROMPT, d), x_sample.reshape(N_LATENT, d)], axis=0)
    cvecs = jnp.zeros((8, d), F32).at[0].set(c_ctx).at[1:1 + DEC_BATCH].set(c)
    mod = modulation_all(cvecs, w_mod, b_mod)[:, :N_GROUPS].reshape(DEPTH, N_GROUPS, 6, 1, d)
    outs = {}
    for layer in range(DEPTH):
        kind, j = layer % N_MIXERS, layer // N_MIXERS
        sh1, sc1, g1, sh2, sc2, g2 = [mod[layer, :, i] for i in range(6)]
        ng = norm_g[layer]
        if kind == 0:
            w_in = mlstm_w_in[j]
            n_main = 2 * MLSTM_HEADS * MLSTM_DQK + 2 * MLSTM_HEADS * MLSTM_DV
            proj = norm_linear(h, ng[0], sc1, sh1, w_in, n_out=n_main, name="mlstm_in")
            w_gates = jnp.zeros((d, LANE), F32).at[:, :4 * MLSTM_HEADS].set(w_in[:, n_main:])
            gates = norm_linear(h, ng[0], sc1, sh1, w_gates, tn=LANE, name="mlstm_gates")
            a, st_c, st_n, st_m = mlstm_scan(proj, gates, mlstm_b_gate[j], mlstm_g_head[j],
                                             n_batch=BATCH, seq=SEQ, row0=0)
            a = mlstm_scan(proj, gates, mlstm_b_gate[j], mlstm_g_head[j], n_batch=DEC_BATCH, seq=DEC_SEQ,
                           row0=N_PROMPT, state=(state_mlstm_c[:, j], state_mlstm_n[:, j], state_mlstm_m[:, j]),
                           into=a)[0]
            outs['mc'] = st_c[:, None]
            outs['mn'] = st_n[:, None, :, :, 0]
            outs['mm'] = st_m[:, None, :, :, 0, 0]
            h = linear_out(a, mlstm_w_out[j], h, ng[1], g1, name="mlstm_out")
        elif kind == 1:
            nh = NAT_HEADS
            hw = nh * NAT_HD
            qkv = norm_linear(h, ng[0], sc1, sh1, nat_w_qkv[j], name="nat_qkv")
            a = attention(qkv, qkv, qkv, n_batch=BATCH, n_heads=nh, nq=SEQ, nk=SEQ, tq=SEQ, row0=0,
                          q_col=lambda hh: hh, k_col=lambda hh: nh + hh, v_col=lambda hh: 2 * nh + hh,
                          name="nat_ctx_attn")
            a = nat_latent(qkv, cache_nat_k[:, j].reshape(DEC_BATCH, PAST_LEN, hw),
                           cache_nat_v[:, j].reshape(DEC_BATCH, PAST_LEN, hw), nat_bias_slabs(nat_rpb[j]), a)
            outs['nk'] = qkv[:N_PROMPT, hw:2 * hw].reshape(BATCH, 1, SEQ, nh, NAT_HD)
            outs['nv'] = qkv[:N_PROMPT, 2 * hw:].reshape(BATCH, 1, SEQ, nh, NAT_HD)
            h = linear_out(a, nat_w_out[j], h, ng[1], g1, name="nat_out")
        elif kind == 2:
            nqk = GQA_HEADS + GQA_KV_HEADS
            grp = GQA_HEADS // GQA_KV_HEADS
            kw = GQA_KV_HEADS * GQA_HD
            qkv = norm_linear(h, ng[0], sc1, sh1, gqa_w_qkv[j], name="gqa_qkv")
            gains = jnp.concatenate([jnp.tile(gqa_q_g[j], GQA_HEADS), jnp.tile(gqa_k_g[j], GQA_KV_HEADS)])
            cos, sa, sb = rope_tables()
            qk = qk_norm_rope(qkv, gains.reshape(1, -1), cos, sa, sb, n_heads=nqk)
            cols = dict(q_col=lambda hh: hh, k_col=lambda hh: GQA_HEADS + hh // grp,
                        v_col=lambda hh: nqk + hh // grp)
            a = attention(qk, qk, qkv, n_batch=BATCH, n_heads=GQA_HEADS, nq=SEQ, nk=SEQ, tq=SEQ, row0=0,
                          name="gqa_ctx_attn", **cols)
            a = attention(qk, qk, qkv, n_batch=DEC_BATCH, n_heads=GQA_HEADS, nq=DEC_SEQ, nk=DEC_SEQ, tq=256,
                          row0=N_PROMPT, k_cache=cache_gqa_k[:, j].reshape(DEC_BATCH, PAST_LEN, kw),
                          v_cache=cache_gqa_v[:, j].reshape(DEC_BATCH, PAST_LEN, kw),
                          kv_of_head=lambda hh: hh // grp, into=a, name="gqa_lat_attn", **cols)
            outs['gk'] = qk[:N_PROMPT, GQA_HEADS * GQA_HD:].reshape(BATCH, 1, SEQ, GQA_KV_HEADS, GQA_HD)
            outs['gv'] = qkv[:N_PROMPT, GQA_HEADS * GQA_HD + kw:].reshape(BATCH, 1, SEQ, GQA_KV_HEADS, GQA_HD)
            h = linear_out(a, gqa_w_out[j], h, ng[1], g1, name="gqa_out")
        else:
            proj = norm_linear(h, ng[0], sc1, sh1, ssd_w_in[j], tn=384, name="ssd_in")
            conv = dict(col0=SSD_D_INNER)
            xbc = conv_silu(proj, ssd_conv_w[j], ssd_conv_b[j], n_batch=BATCH, seq=SEQ, row0=0, **conv)
            xbc = conv_silu(proj, ssd_conv_w[j], ssd_conv_b[j], n_batch=DEC_BATCH, seq=DEC_SEQ, row0=N_PROMPT,
                            into=xbc, **conv)
            y, ssq, st_s = ssd_scan(proj, xbc, ssd_dt_bias[j], ssd_a_log[j], ssd_d[j],
                                    n_batch=BATCH, seq=SEQ, row0=0)
            y, ssq = ssd_scan(proj, xbc, ssd_dt_bias[j], ssd_a_log[j], ssd_d[j], n_batch=DEC_BATCH, seq=DEC_SEQ,
                              row0=N_PROMPT, state=state_ssd[:, j], into=(y, ssq))
            outs['ss'] = st_s[:, None]
            h = linear_out(y, ssd_w_out[j], h, ng[1], g1, ssq=ssq, gn=ssd_g_norm[j], name="ssd_out")
        e = layer // 2
        if layer % 2 == 0:
            act = norm_swiglu_in(h, ng[2], sc2, sh2, ffn_w_in[e])
            h = linear_out(act, ffn_w_out[e], h, ng[3], g2, name="ffn_out")
        else:
            h = moe_layer(h, ng[2], sc2, sh2, moe_w_router[e], moe_b_router[e], moe_w_in[e], moe_w_out[e],
                          ng[3], g2)
    y_prompt = h[:N_PROMPT].reshape(BATCH, SEQ, d)
    y_sample = h[N_PROMPT:].reshape(DEC_BATCH, DEC_SEQ, d)
    return (y_prompt, y_sample, outs['mc'], outs['mn'], outs['mm'], outs['nk'], outs['nv'],
            outs['gk'], outs['gv'], outs['ss'])
```

```python
import functools

import jax
import jax.numpy as jnp
from jax import lax
from jax.experimental import pallas as pl
from jax.experimental.pallas import tpu as pltpu

D_MODEL = 2048
BATCH = 16
SEQ = 256
DEPTH = 4
DEC_BATCH = 2
DEC_SEQ = 2048
PAST_LEN = 512
GRID_W = 64
N_MIXERS = 4

MLSTM_HEADS = 8
MLSTM_DV = D_MODEL // MLSTM_HEADS
MLSTM_DQK = MLSTM_DV // 2
MLSTM_CHUNK = 64

NAT_HEADS = 16
NAT_HD = D_MODEL // NAT_HEADS
NAT_KH = 8
NAT_KW = 16

GQA_HEADS = 16
GQA_KV_HEADS = 4
GQA_HD = D_MODEL // GQA_HEADS
ROPE_THETA = 10000.0

SSD_D_INNER = 2 * D_MODEL
SSD_HEADDIM = 64
SSD_HEADS = SSD_D_INNER // SSD_HEADDIM
SSD_GROUPS = 8
SSD_STATE = 128
SSD_CONV = 5
SSD_CHUNK = 128
SSD_R = SSD_HEADS // SSD_GROUPS
SSD_GW = SSD_R * SSD_HEADDIM

D_FF = 7 * D_MODEL // 2
N_EXPERTS = 8
TOP_K = 2

EPS = 1e-6
NEG = -1e30

N_PROMPT = BATCH * SEQ
N_LATENT = DEC_BATCH * DEC_SEQ
N_TOK = N_PROMPT + N_LATENT
N_GROUPS = 1 + DEC_BATCH
LANE = 128
VMEM_LIMIT = 56 * 1024 * 1024

F32 = jnp.float32
BF16 = jnp.bfloat16
HIGHEST = lax.Precision.HIGHEST
NT = (((1,), (1,)), ((), ()))
TN = (((0,), (0,)), ((), ()))


def _params(sem, vmem=VMEM_LIMIT):
    return pltpu.CompilerParams(dimension_semantics=sem, vmem_limit_bytes=vmem)


def _group_of_tile(i, tm):
    return jnp.maximum(i * tm // DEC_SEQ - (N_PROMPT // DEC_SEQ - 1), 0)


def _rms(x, g):
    return x * lax.rsqrt(jnp.mean(x * x, axis=-1, keepdims=True) + EPS) * g


def _softplus(x):
    return jnp.maximum(x, 0.0) + jnp.log(1.0 + jnp.exp(-jnp.abs(x)))


def _iota(shape, dim):
    return lax.broadcasted_iota(jnp.int32, shape, dim)


def _mod_kernel(c_ref, w_ref, b_ref, o_ref):
    c = c_ref[...]
    s = (c * jax.nn.sigmoid(c)).astype(BF16)
    o_ref[0] = jnp.dot(s, w_ref[0].astype(BF16), preferred_element_type=F32) + b_ref[0]


def modulation_all(cvecs, w_mod, b_mod):
    tn = 1024
    n = w_mod.shape[-1]
    return pl.pallas_call(
        _mod_kernel,
        grid=(DEPTH, n // tn),
        in_specs=[pl.BlockSpec((8, D_MODEL), lambda l, j: (0, 0)),
                  pl.BlockSpec((1, D_MODEL, tn), lambda l, j: (l, 0, j)),
                  pl.BlockSpec((1, 1, tn), lambda l, j: (l, 0, j))],
        out_specs=pl.BlockSpec((1, 8, tn), lambda l, j: (l, 0, j)),
        out_shape=jax.ShapeDtypeStruct((DEPTH, 8, n), F32),
        compiler_params=_params(("parallel", "parallel")),
        name="modulation",
    )(cvecs, w_mod, b_mod.reshape(DEPTH, 1, n))


def _norm_linear_kernel(x_ref, g_ref, sc_ref, sh_ref, w_ref, o_ref, u_ref):
    @pl.when(pl.program_id(1) == 0)
    def _():
        u = _rms(x_ref[...], g_ref[...]) * (1.0 + sc_ref[0]) + sh_ref[0]
        u_ref[...] = u.astype(BF16)

    o_ref[...] = jnp.dot(u_ref[...], w_ref[0].astype(BF16), preferred_element_type=F32).astype(o_ref.dtype)


def norm_linear(h, g, scale, shift, w, wl, *, n_out=None, tm=1024, tn=512, out_dtype=F32, name="norm_linear"):
    m, d = h.shape
    n_out = w.shape[2] if n_out is None else n_out
    return pl.pallas_call(
        _norm_linear_kernel,
        grid=(m // tm, pl.cdiv(n_out, tn)),
        in_specs=[pl.BlockSpec((tm, d), lambda i, j: (i, 0)),
                  pl.BlockSpec((1, d), lambda i, j: (0, 0)),
                  pl.BlockSpec((1, 1, d), lambda i, j: (_group_of_tile(i, tm), 0, 0)),
                  pl.BlockSpec((1, 1, d), lambda i, j: (_group_of_tile(i, tm), 0, 0)),
                  pl.BlockSpec((1, d, tn), lambda i, j: (wl, 0, j))],
        out_specs=pl.BlockSpec((tm, tn), lambda i, j: (i, j)),
        out_shape=jax.ShapeDtypeStruct((m, n_out), out_dtype),
        scratch_shapes=[pltpu.VMEM((tm, d), BF16)],
        compiler_params=_params(("parallel", "arbitrary")),
        name=name,
    )(h, g.reshape(1, d), scale, shift, w)


def _norm_swiglu_kernel(x_ref, g_ref, sc_ref, sh_ref, wg_ref, wu_ref, o_ref, u_ref):
    @pl.when(pl.program_id(1) == 0)
    def _():
        u = _rms(x_ref[...], g_ref[...]) * (1.0 + sc_ref[0]) + sh_ref[0]
        u_ref[...] = u.astype(BF16)

    u = u_ref[...]
    a = jnp.dot(u, wg_ref[0].astype(BF16), preferred_element_type=F32)
    b = jnp.dot(u, wu_ref[0].astype(BF16), preferred_element_type=F32)
    o_ref[...] = (a * jax.nn.sigmoid(a) * b).astype(o_ref.dtype)


def norm_swiglu_in(h, g, scale, shift, w_in, wl, *, tm=1024, tn=256):
    m, d = h.shape
    f = w_in.shape[2] // 2
    nj = f // tn
    return pl.pallas_call(
        _norm_swiglu_kernel,
        grid=(m // tm, nj),
        in_specs=[pl.BlockSpec((tm, d), lambda i, j: (i, 0)),
                  pl.BlockSpec((1, d), lambda i, j: (0, 0)),
                  pl.BlockSpec((1, 1, d), lambda i, j: (_group_of_tile(i, tm), 0, 0)),
                  pl.BlockSpec((1, 1, d), lambda i, j: (_group_of_tile(i, tm), 0, 0)),
                  pl.BlockSpec((1, d, tn), lambda i, j: (wl, 0, j)),
                  pl.BlockSpec((1, d, tn), lambda i, j: (wl, 0, j + nj))],
        out_specs=pl.BlockSpec((tm, tn), lambda i, j: (i, j)),
        out_shape=jax.ShapeDtypeStruct((m, f), BF16),
        scratch_shapes=[pltpu.VMEM((tm, d), BF16)],
        compiler_params=_params(("parallel", "arbitrary")),
        name="norm_swiglu_in",
    )(h, g.reshape(1, d), scale, shift, w_in, w_in)


def _linear_out_kernel(*refs, prescale, kdim, rc):
    if prescale:
        a_ref, w_ref, h_ref, g_ref, gate_ref, ssq_ref, gn_ref, o_ref = refs
    else:
        a_ref, w_ref, h_ref, g_ref, gate_ref, o_ref = refs
    k = pl.program_id(1)
    n_chunks = o_ref.shape[0] // rc
    wb = w_ref[0].astype(BF16)

    @pl.when(k == 0)
    def _():
        o_ref[...] = jnp.zeros_like(o_ref)

    def accumulate(r, c):
        rows = pl.ds(pl.multiple_of(r * rc, rc), rc)
        a = a_ref[rows, :]
        if prescale:
            a = a * lax.rsqrt(ssq_ref[rows, :1] * (1.0 / kdim) + EPS) * gn_ref[...]
        o_ref[rows, :] += jnp.dot(a.astype(BF16), wb, preferred_element_type=F32)
        return c

    lax.fori_loop(0, n_chunks, accumulate, 0)

    @pl.when(k == pl.num_programs(1) - 1)
    def _():
        def finish(r, c):
            rows = pl.ds(pl.multiple_of(r * rc, rc), rc)
            o_ref[rows, :] = h_ref[rows, :] + gate_ref[0] * _rms(o_ref[rows, :], g_ref[...])
            return c

        lax.fori_loop(0, n_chunks, finish, 0)


def linear_out(a, w, wl, h, g, gate, *, ssq=None, gn=None, tm=1024, tk=256, rc=256, name="linear_out"):
    m, kdim = a.shape
    d = w.shape[2]
    prescale = ssq is not None
    in_specs = [pl.BlockSpec((tm, tk), lambda i, k: (i, k)),
                pl.BlockSpec((1, tk, d), lambda i, k: (wl, k, 0)),
                pl.BlockSpec((tm, d), lambda i, k: (i, 0), pipeline_mode=pl.Buffered(1)),
                pl.BlockSpec((1, d), lambda i, k: (0, 0)),
                pl.BlockSpec((1, 1, d), lambda i, k: (_group_of_tile(i, tm), 0, 0))]
    args = [a, w, h, g.reshape(1, d), gate]
    if prescale:
        in_specs += [pl.BlockSpec((tm, LANE), lambda i, k: (i, 0)),
                     pl.BlockSpec((1, tk), lambda i, k: (0, k))]
        args += [ssq, gn.reshape(1, kdim)]
    return pl.pallas_call(
        functools.partial(_linear_out_kernel, prescale=prescale, kdim=kdim, rc=rc),
        grid=(m // tm, kdim // tk),
        in_specs=in_specs,
        out_specs=pl.BlockSpec((tm, d), lambda i, k: (i, 0)),
        out_shape=jax.ShapeDtypeStruct((m, d), F32),
        compiler_params=_params(("parallel", "arbitrary")),
        name=name,
    )(*args)


def _norm_router_kernel(x_ref, g_ref, sc_ref, sh_ref, wr_ref, br_ref, u_ref, rt_ref):
    u = _rms(x_ref[...], g_ref[...]) * (1.0 + sc_ref[0]) + sh_ref[0]
    u_ref[...] = u
    lg = jnp.dot(u, wr_ref[...], preferred_element_type=F32, precision=HIGHEST) + br_ref[...]
    lane = _iota(lg.shape, 1)
    lanef = lane.astype(F32)
    lg = jnp.where(lane < N_EXPERTS, lg, -jnp.inf)
    m1 = jnp.max(lg, axis=-1, keepdims=True)
    i1 = jnp.min(jnp.where(lg == m1, lanef, float(LANE)), axis=-1, keepdims=True)
    lg2 = jnp.where(lanef == i1, -jnp.inf, lg)
    m2 = jnp.max(lg2, axis=-1, keepdims=True)
    i2 = jnp.min(jnp.where(lg2 == m2, lanef, float(LANE)), axis=-1, keepdims=True)
    e = jnp.exp(m2 - m1)
    w1 = 1.0 / (1.0 + e)
    w2 = e / (1.0 + e)
    rt_ref[...] = jnp.where(lane == 0, i1, jnp.where(lane == 1, i2, jnp.where(lane == 2, w1,
                            jnp.where(lane == 3, w2, 0.0))))


def norm_router(h, g, scale, shift, w_router, b_router, *, tm=512):
    m, d = h.shape
    wr = jnp.zeros((d, LANE), F32).at[:, :N_EXPERTS].set(w_router)
    br = jnp.zeros((1, LANE), F32).at[0, :N_EXPERTS].set(b_router)
    return pl.pallas_call(
        _norm_router_kernel,
        grid=(m // tm,),
        in_specs=[pl.BlockSpec((tm, d), lambda i: (i, 0)),
                  pl.BlockSpec((1, d), lambda i: (0, 0)),
                  pl.BlockSpec((1, 1, d), lambda i: (_group_of_tile(i, tm), 0, 0)),
                  pl.BlockSpec((1, 1, d), lambda i: (_group_of_tile(i, tm), 0, 0)),
                  pl.BlockSpec((d, LANE), lambda i: (0, 0)),
                  pl.BlockSpec((1, LANE), lambda i: (0, 0))],
        out_specs=[pl.BlockSpec((tm, d), lambda i: (i, 0)),
                   pl.BlockSpec((tm, LANE), lambda i: (i, 0))],
        out_shape=[jax.ShapeDtypeStruct((m, d), F32), jax.ShapeDtypeStruct((m, LANE), F32)],
        compiler_params=_params(("parallel",)),
        name="norm_router",
    )(h, g.reshape(1, d), scale, shift, wr, br)


MOE_TM = 512


def _row_copy(src_hbm, idx, dst, r, sem):
    return pltpu.make_async_copy(src_hbm.at[pl.ds(idx, 1), :], dst.at[pl.ds(r, 1), :], sem)


def _gather_all(idx_ref, base, src_hbm, dst, sem, n):
    def issue(r, c):
        _row_copy(src_hbm, idx_ref[base + r], dst, r, sem).start()
        return c

    lax.fori_loop(0, n, issue, 0, unroll=8)

    def drain(r, c):
        _row_copy(src_hbm, 0, dst, r, sem).wait()
        return c

    lax.fori_loop(0, n, drain, 0, unroll=8)


def _moe_gather_kernel(tok_ref, nu_ref, u_hbm, o_ref, buf, sem):
    @pl.when(pl.program_id(0) < nu_ref[0])
    def _():
        _gather_all(tok_ref, pl.program_id(0) * MOE_TM, u_hbm, buf, sem, MOE_TM)
        o_ref[...] = buf[...].astype(o_ref.dtype)

    @pl.when(pl.program_id(0) >= nu_ref[0])
    def _():
        o_ref[...] = jnp.zeros_like(o_ref)


def moe_gather(u, row_tok, n_used):
    r = row_tok.shape[0]
    d = u.shape[1]
    return pl.pallas_call(
        _moe_gather_kernel,
        grid_spec=pltpu.PrefetchScalarGridSpec(
            num_scalar_prefetch=2,
            grid=(r // MOE_TM,),
            in_specs=[pl.BlockSpec(memory_space=pl.ANY)],
            out_specs=pl.BlockSpec((MOE_TM, d), lambda b, tok, nu: (b, 0)),
            scratch_shapes=[pltpu.VMEM((MOE_TM, d), F32), pltpu.SemaphoreType.DMA(())],
        ),
        out_shape=jax.ShapeDtypeStruct((r, d), BF16),
        compiler_params=_params(("arbitrary",)),
        name="moe_gather",
    )(row_tok, n_used, u)


def _moe_in_kernel(be_ref, nu_ref, x_ref, wg_ref, wu_ref, o_ref):
    @pl.when(pl.program_id(1) < nu_ref[0])
    def _():
        x = x_ref[...]
        a = jnp.dot(x, wg_ref[0, 0].astype(BF16), preferred_element_type=F32)
        b = jnp.dot(x, wu_ref[0, 0].astype(BF16), preferred_element_type=F32)
        o_ref[...] = (a * jax.nn.sigmoid(a) * b).astype(o_ref.dtype)

    @pl.when(pl.program_id(1) >= nu_ref[0])
    def _():
        o_ref[...] = jnp.zeros_like(o_ref)


def _used_block(b, nu):
    return jnp.minimum(b, nu[0] - 1)


def moe_in(x_rows, blk_expert, n_used, w_in, wl, *, tn=512):
    r, d = x_rows.shape
    f = w_in.shape[3] // 2
    nj = f // tn
    nb = r // MOE_TM
    return pl.pallas_call(
        _moe_in_kernel,
        grid_spec=pltpu.PrefetchScalarGridSpec(
            num_scalar_prefetch=2,
            grid=(nj, nb),
            in_specs=[pl.BlockSpec((MOE_TM, d), lambda j, b, be, nu: (_used_block(b, nu), 0)),
                      pl.BlockSpec((1, 1, d, tn), lambda j, b, be, nu: (wl, be[_used_block(b, nu)], 0, j)),
                      pl.BlockSpec((1, 1, d, tn), lambda j, b, be, nu: (wl, be[_used_block(b, nu)], 0, j + nj))],
            out_specs=pl.BlockSpec((MOE_TM, tn), lambda j, b, be, nu: (b, j)),
        ),
        out_shape=jax.ShapeDtypeStruct((r, f), BF16),
        compiler_params=_params(("arbitrary", "arbitrary")),
        name="moe_in",
    )(blk_expert, n_used, x_rows, w_in, w_in)


def _moe_out_kernel(be_ref, nu_ref, a_ref, w_ref, rw_ref, o_ref):
    @pl.when(pl.program_id(1) < nu_ref[0])
    def _():
        y = jnp.dot(a_ref[...], w_ref[0, 0].astype(BF16), preferred_element_type=F32)
        o_ref[...] = y * rw_ref[...]

    @pl.when(pl.program_id(1) >= nu_ref[0])
    def _():
        o_ref[...] = jnp.zeros_like(o_ref)


def moe_out(act_rows, blk_expert, n_used, w_out, wl, row_w, *, tn=256):
    r, f = act_rows.shape
    d = w_out.shape[3]
    nb = r // MOE_TM
    return pl.pallas_call(
        _moe_out_kernel,
        grid_spec=pltpu.PrefetchScalarGridSpec(
            num_scalar_prefetch=2,
            grid=(d // tn, nb),
            in_specs=[pl.BlockSpec((MOE_TM, f), lambda j, b, be, nu: (_used_block(b, nu), 0)),
                      pl.BlockSpec((1, 1, f, tn), lambda j, b, be, nu: (wl, be[_used_block(b, nu)], 0, j)),
                      pl.BlockSpec((MOE_TM, 1), lambda j, b, be, nu: (_used_block(b, nu), 0))],
            out_specs=pl.BlockSpec((MOE_TM, tn), lambda j, b, be, nu: (b, j)),
        ),
        out_shape=jax.ShapeDtypeStruct((r, d), F32),
        compiler_params=_params(("arbitrary", "arbitrary")),
        name="moe_out",
    )(blk_expert, n_used, act_rows, w_out, row_w.reshape(r, 1))


COMBINE_TM = 256


def _moe_combine_kernel(pos_ref, y_hbm, h_ref, g_ref, gate_ref, o_ref, buf, sem):
    n = TOP_K * COMBINE_TM
    _gather_all(pos_ref, pl.program_id(0) * n, y_hbm, buf, sem, n)
    f = buf[:COMBINE_TM, :] + buf[COMBINE_TM:, :]
    o_ref[...] = h_ref[...] + gate_ref[0] * _rms(f, g_ref[...])


def moe_combine(y_rows, pos_blocked, h, g, gate):
    m, d = h.shape
    tm = COMBINE_TM
    return pl.pallas_call(
        _moe_combine_kernel,
        grid_spec=pltpu.PrefetchScalarGridSpec(
            num_scalar_prefetch=1,
            grid=(m // tm,),
            in_specs=[pl.BlockSpec(memory_space=pl.ANY),
                      pl.BlockSpec((tm, d), lambda i, pos: (i, 0)),
                      pl.BlockSpec((1, d), lambda i, pos: (0, 0)),
                      pl.BlockSpec((1, 1, d), lambda i, pos: (_group_of_tile(i, tm), 0, 0))],
            out_specs=pl.BlockSpec((tm, d), lambda i, pos: (i, 0)),
            scratch_shapes=[pltpu.VMEM((TOP_K * tm, d), F32), pltpu.SemaphoreType.DMA(())],
        ),
        out_shape=jax.ShapeDtypeStruct((m, d), F32),
        compiler_params=_params(("arbitrary",)),
        name="moe_combine",
    )(pos_blocked, y_rows, h, g.reshape(1, d), gate)


def moe_layer(h, g, scale, shift, w_router, b_router, w_in, w_out, wl, g_post, gate):
    t = h.shape[0]
    u, route = norm_router(h, g, scale, shift, w_router, b_router)
    expert = route[:, :TOP_K].astype(jnp.int32).reshape(-1)
    weight = route[:, TOP_K:2 * TOP_K].reshape(-1)
    n_assign = t * TOP_K
    onehot = (expert[:, None] == jnp.arange(N_EXPERTS, dtype=jnp.int32)[None, :]).astype(jnp.int32)
    rank = jnp.sum((jnp.cumsum(onehot, axis=0) - onehot) * onehot, axis=1)
    counts = jnp.sum(onehot, axis=0)
    padded = (counts + MOE_TM - 1) // MOE_TM * MOE_TM
    pad_end = jnp.cumsum(padded)
    pad_start = pad_end - padded
    pos = (pad_start[expert] + rank).astype(jnp.int32)
    n_blocks = n_assign // MOE_TM + N_EXPERTS
    n_rows = n_blocks * MOE_TM
    token = jnp.repeat(jnp.arange(t, dtype=jnp.int32), TOP_K)
    row_tok = jnp.zeros((n_rows,), jnp.int32).at[pos].set(token)
    row_w = jnp.zeros((n_rows,), F32).at[pos].set(weight)
    blk_expert = jnp.minimum(
        jnp.searchsorted(pad_end, jnp.arange(n_blocks, dtype=jnp.int32) * MOE_TM, side='right'),
        N_EXPERTS - 1).astype(jnp.int32)
    n_used = (pad_end[-1] // MOE_TM).astype(jnp.int32).reshape(1)
    pos_blocked = pos.reshape(t // COMBINE_TM, COMBINE_TM, TOP_K).transpose(0, 2, 1).reshape(-1)

    x_rows = moe_gather(u, row_tok, n_used)
    act = moe_in(x_rows, blk_expert, n_used, w_in, wl)
    y_rows = moe_out(act, blk_expert, n_used, w_out, wl, row_w)
    return moe_combine(y_rows, pos_blocked, h, g_post, gate)


def _attn_kernel(*refs, scale, two, aliased):
    if aliased:
        refs = refs[1:]
    if two:
        q_ref, k_ref, v_ref, kc_ref, vc_ref, o_ref = refs
    else:
        q_ref, k_ref, v_ref, o_ref = refs
    q = q_ref[...].astype(BF16)
    s = lax.dot_general(q, k_ref[...].astype(BF16), NT, preferred_element_type=F32) * scale
    m = jnp.max(s, axis=-1, keepdims=True)
    if two:
        s2 = lax.dot_general(q, kc_ref[0].astype(BF16), NT, preferred_element_type=F32) * scale
        m = jnp.maximum(m, jnp.max(s2, axis=-1, keepdims=True))
    p = jnp.exp(s - m)
    l = jnp.sum(p, axis=-1, keepdims=True)
    o = jnp.dot(p.astype(BF16), v_ref[...].astype(BF16), preferred_element_type=F32)
    if two:
        p2 = jnp.exp(s2 - m)
        l = l + jnp.sum(p2, axis=-1, keepdims=True)
        o = o + jnp.dot(p2.astype(BF16), vc_ref[0].astype(BF16), preferred_element_type=F32)
    o_ref[...] = (o / l).astype(o_ref.dtype)


def attention(q_arr, k_arr, v_arr, *, n_batch, n_heads, nq, nk, tq, row0, q_col, k_col, v_col,
              k_cache=None, v_cache=None, kv_of_head=lambda h: h, into=None, name="attention"):
    hd = LANE
    nqb = nq // tq
    two = k_cache is not None
    aliased = into is not None
    in_specs = [pl.BlockSpec((tq, hd), lambda b, h, i: (row0 // tq + b * nqb + i, q_col(h))),
                pl.BlockSpec((nk, hd), lambda b, h, i: (row0 // nk + b, k_col(h))),
                pl.BlockSpec((nk, hd), lambda b, h, i: (row0 // nk + b, v_col(h)))]
    args = [q_arr, k_arr, v_arr]
    if two:
        n_past = k_cache.shape[1]
        in_specs += [pl.BlockSpec((1, n_past, hd), lambda b, h, i: (b, 0, kv_of_head(h))),
                     pl.BlockSpec((1, n_past, hd), lambda b, h, i: (b, 0, kv_of_head(h)))]
        args += [k_cache, v_cache]
    if aliased:
        in_specs = [pl.BlockSpec(memory_space=pl.ANY)] + in_specs
        args = [into] + args
    return pl.pallas_call(
        functools.partial(_attn_kernel, scale=hd ** -0.5, two=two, aliased=aliased),
        grid=(n_batch, n_heads, nqb),
        in_specs=in_specs,
        out_specs=pl.BlockSpec((tq, hd), lambda b, h, i: (row0 // tq + b * nqb + i, h)),
        out_shape=jax.ShapeDtypeStruct((N_TOK, n_heads * hd), BF16),
        input_output_aliases={0: 0} if aliased else {},
        compiler_params=_params(("parallel", "parallel", "arbitrary")),
        name=name,
    )(*args)


def _nat_latent_kernel(into_ref, q_ref, k_ref, v_ref, kc_ref, vc_ref, b_ref, o_ref, kb_ref, vb_ref, *, rows):
    del into_ref
    w, kh = GRID_W, NAT_KH
    nwin = kh * w
    half = kh // 2
    scale = NAT_HD ** -0.5
    qi = _iota((w, nwin), 0)
    kk = _iota((w, nwin), 1) % w
    c_start = jnp.clip(qi - NAT_KW // 2, 0, w - NAT_KW)
    col_ok = (kk >= c_start) & (kk < c_start + NAT_KW)
    kb_ref[...] = k_ref[...].astype(BF16)
    vb_ref[...] = v_ref[...].astype(BF16)
    kc = kc_ref[0].astype(BF16)
    vc = vc_ref[0].astype(BF16)

    def one_row(r, start, rel):
        q0 = r * w
        k0 = start * w
        if not isinstance(r, int):
            q0 = pl.multiple_of(q0, w)
            k0 = pl.multiple_of(k0, w)
        q = q_ref[pl.ds(q0, w), :].astype(BF16)
        s1 = lax.dot_general(q, kb_ref[pl.ds(k0, nwin), :], NT, preferred_element_type=F32) * scale + b_ref[0, rel]
        s1 = jnp.where(col_ok, s1, NEG)
        s2 = lax.dot_general(q, kc, NT, preferred_element_type=F32) * scale
        m = jnp.maximum(jnp.max(s1, axis=-1, keepdims=True), jnp.max(s2, axis=-1, keepdims=True))
        p1 = jnp.exp(s1 - m)
        p2 = jnp.exp(s2 - m)
        l = jnp.sum(p1, axis=-1, keepdims=True) + jnp.sum(p2, axis=-1, keepdims=True)
        o = (jnp.dot(p1.astype(BF16), vb_ref[pl.ds(k0, nwin), :], preferred_element_type=F32)
             + jnp.dot(p2.astype(BF16), vc, preferred_element_type=F32))
        o_ref[pl.ds(q0, w), :] = (o / l).astype(o_ref.dtype)

    last_start = rows - kh
    for r in range(half):
        one_row(r, 0, kh - 1 - r)

    def interior(r, c):
        one_row(r, r - half, kh - 1 - half)
        return c

    lax.fori_loop(half, last_start + half + 1, interior, 0)
    for r in range(last_start + half + 1, rows):
        one_row(r, last_start, last_start - r + kh - 1)


def nat_bias_slabs(rpb):
    c = jnp.arange(GRID_W)
    col_off = jnp.clip(c[None, :] - c[:, None], -(NAT_KW - 1), NAT_KW - 1) + NAT_KW - 1
    t = rpb[:, :, col_off]
    rel = jnp.arange(NAT_KH)[:, None] + jnp.arange(NAT_KH)[None, :]
    return t[:, rel].transpose(0, 1, 3, 2, 4).reshape(rpb.shape[0], NAT_KH, GRID_W, NAT_KH * GRID_W)


def nat_latent(qkv, k_cache, v_cache, bias, into):
    nh, hd = NAT_HEADS, NAT_HD
    rows = DEC_SEQ // GRID_W
    rb0 = N_PROMPT // DEC_SEQ
    blk = lambda c0: pl.BlockSpec((DEC_SEQ, hd), lambda b, h: (rb0 + b, c0 + h))
    return pl.pallas_call(
        functools.partial(_nat_latent_kernel, rows=rows),
        grid=(DEC_BATCH, nh),
        in_specs=[pl.BlockSpec(memory_space=pl.ANY), blk(0), blk(nh), blk(2 * nh),
                  pl.BlockSpec((1, PAST_LEN, hd), lambda b, h: (b, 0, h)),
                  pl.BlockSpec((1, PAST_LEN, hd), lambda b, h: (b, 0, h)),
                  pl.BlockSpec((1, NAT_KH, GRID_W, NAT_KH * GRID_W), lambda b, h: (h, 0, 0, 0))],
        out_specs=pl.BlockSpec((DEC_SEQ, hd), lambda b, h: (rb0 + b, h)),
        out_shape=jax.ShapeDtypeStruct((N_TOK, nh * hd), BF16),
        input_output_aliases={0: 0},
        scratch_shapes=[pltpu.VMEM((DEC_SEQ, hd), BF16), pltpu.VMEM((DEC_SEQ, hd), BF16)],
        compiler_params=_params(("parallel", "parallel")),
        name="nat_latent",
    )(into, qkv, qkv, qkv, k_cache, v_cache, bias)


def _qk_norm_rope_kernel(x_ref, g_ref, cos_ref, sa_ref, sb_ref, o_ref, *, n_heads):
    cos, sa, sb = cos_ref[...], sa_ref[...], sb_ref[...]
    quarter = GQA_HD // 4
    for hh in range(n_heads):
        sl = slice(hh * GQA_HD, (hh + 1) * GQA_HD)
        y = _rms(x_ref[:, sl], g_ref[:, sl])
        up = pltpu.roll(y, GQA_HD - quarter, axis=1)
        dn = pltpu.roll(y, quarter, axis=1)
        o_ref[:, sl] = y * cos + up * sa + dn * sb


def qk_norm_rope(qkv, gains, cos, sa, sb, *, n_heads, tm=512):
    m = qkv.shape[0]
    w = n_heads * GQA_HD
    return pl.pallas_call(
        functools.partial(_qk_norm_rope_kernel, n_heads=n_heads),
        grid=(m // tm,),
        in_specs=[pl.BlockSpec((tm, w), lambda i: (i, 0)),
                  pl.BlockSpec((1, w), lambda i: (0, 0)),
                  pl.BlockSpec((tm, GQA_HD), lambda i: (i, 0)),
                  pl.BlockSpec((tm, GQA_HD), lambda i: (i, 0)),
                  pl.BlockSpec((tm, GQA_HD), lambda i: (i, 0))],
        out_specs=pl.BlockSpec((tm, w), lambda i: (i, 0)),
        out_shape=jax.ShapeDtypeStruct((m, w), F32),
        compiler_params=_params(("parallel",)),
        name="qk_norm_rope",
    )(qkv, gains, cos, sa, sb)


def rope_tables():
    t = jnp.arange(DEC_SEQ)
    row = (t // GRID_W).astype(F32)
    col = (t % GRID_W).astype(F32)
    half = GQA_HD // 2
    freqs = ROPE_THETA ** (-jnp.arange(0, half, 2, dtype=F32) / half)
    ar = row[:, None] * freqs
    ac = col[:, None] * freqs
    z = jnp.zeros_like(ar)
    cos = jnp.concatenate([jnp.cos(ar), jnp.cos(ar), jnp.cos(ac), jnp.cos(ac)], axis=-1)
    sa = jnp.concatenate([-jnp.sin(ar), z, -jnp.sin(ac), z], axis=-1)
    sb = jnp.concatenate([z, jnp.sin(ar), z, jnp.sin(ac)], axis=-1)
    ones = jnp.ones((N_PROMPT, GQA_HD), F32)
    zeros = jnp.zeros((N_PROMPT, GQA_HD), F32)
    tile = lambda a: jnp.tile(a, (DEC_BATCH, 1))
    return (jnp.concatenate([ones, tile(cos)], 0), jnp.concatenate([zeros, tile(sa)], 0),
            jnp.concatenate([zeros, tile(sb)], 0))


def _mlstm_kernel(*refs, seq, zero_state):
    if zero_state:
        (q_ref, k_ref, v_ref, og_ref, gt_ref, bg_ref, gh_ref,
         a_ref, cf_ref, nf_ref, mf_ref, hf_ref, hb_ref, cs_ref, ns_ref, m_ref) = refs
    else:
        (into_ref, q_ref, k_ref, v_ref, og_ref, gt_ref, bg_ref, gh_ref, c0_ref, n0_ref, m0_ref,
         a_ref, hf_ref, hb_ref, cs_ref, ns_ref, m_ref) = refs
        del into_ref
    L = MLSTM_CHUNK
    nc = seq // L
    head = pl.program_id(1)
    scale = MLSTM_DQK ** -0.5

    if zero_state:
        cs_ref[...] = jnp.zeros_like(cs_ref)
        ns_ref[...] = jnp.zeros_like(ns_ref)
        m_ref[...] = jnp.full_like(m_ref, NEG)
    else:
        cs_ref[...] = c0_ref[0, :, 0]
        ns_ref[...] = n0_ref[0, :, 0]
        m_ref[...] = m0_ref[0, :, 0]

    rr = _iota((LANE, LANE), 0)
    cc = _iota((LANE, LANE), 1)
    sel = jnp.where((cc < 4) & (rr == MLSTM_HEADS * cc + head), 1.0, 0.0).astype(F32)
    lane = _iota((L, LANE), 1)
    jj = _iota((L, L), 0)
    ss = _iota((L, L), 1)
    lower = (ss <= jj)
    upper = (ss >= jj)
    lowerf = lower.astype(F32)

    def chunk_gates(c0):
        g = gt_ref[pl.ds(c0, L), :] + bg_ref[...]
        x = jnp.dot(g, sel, preferred_element_type=F32, precision=HIGHEST)
        logsig = jnp.minimum(x, 0.0) - jnp.log(1.0 + jnp.exp(-jnp.abs(x)))
        x = jnp.where((lane == 1) | (lane == 3), logsig, x)
        pre = jnp.dot(lowerf, x, preferred_element_type=F32, precision=HIGHEST)
        tot = pre[L - 1:L, :]
        suf = tot - pre + x
        y = jnp.where(lane == 1, pre, jnp.where(lane == 3, suf, x))
        return y, y.T, tot

    def direction(d, c0):
        y, yt, tot = chunk_gates(c0)
        li_col, b_col = y[:, 2 * d:2 * d + 1], y[:, 2 * d + 1:2 * d + 2]
        li_row, b_row = yt[2 * d:2 * d + 1, :], yt[2 * d + 1:2 * d + 2, :]
        bl = tot[:, 2 * d + 1:2 * d + 2]
        m = m_ref[d][:, :1]
        dlog = jnp.where(lower if d == 0 else upper, b_col - b_row + li_row, -jnp.inf)
        inter = b_col + m
        mj = jnp.maximum(inter, jnp.max(dlog, axis=-1, keepdims=True))
        dw = jnp.exp(dlog - mj)
        iw = jnp.exp(inter - mj)
        qc = q_ref[pl.ds(c0, L), :] * scale
        kc = k_ref[pl.ds(c0, L), :]
        qb = qc.astype(BF16)
        vb = v_ref[pl.ds(c0, L), :].astype(BF16)
        sc = lax.dot_general(qb, kc.astype(BF16), NT, preferred_element_type=F32) * dw
        cs = cs_ref[d]
        ns = ns_ref[d]
        num = (jnp.dot(sc.astype(BF16), vb, preferred_element_type=F32)
               + iw * jnp.dot(qb, cs.astype(BF16), preferred_element_type=F32))
        den = jnp.sum(sc, axis=-1, keepdims=True) + iw * jnp.sum(qc * ns, axis=-1, keepdims=True)
        hc = num / jnp.maximum(jnp.abs(den), jnp.exp(-mj))
        elog = bl - b_col + li_col
        carry_log = bl + m
        m_new = jnp.maximum(carry_log, jnp.max(elog, axis=0, keepdims=True))
        ew = jnp.exp(elog - m_new)
        cw = jnp.exp(carry_log - m_new)
        kw = ew * kc
        cs_ref[d] = cw * cs + lax.dot_general(kw.astype(BF16), vb, TN, preferred_element_type=F32)
        ns_ref[d] = cw * ns + jnp.sum(kw, axis=0, keepdims=True)
        m_ref[d] = jnp.broadcast_to(m_new, (1, LANE))
        return hc

    def body(c, carry):
        cf = pl.multiple_of(c * L, L)
        cb = pl.multiple_of((nc - 1 - c) * L, L)
        hf_ref[pl.ds(cf, L), :] = direction(0, cf)
        hb_ref[pl.ds(cb, L), :] = direction(1, cb)
        return carry

    lax.fori_loop(0, nc, body, 0)
    hsum = _rms(hf_ref[...] + hb_ref[...], gh_ref[...])
    a_ref[...] = (hsum * jax.nn.sigmoid(og_ref[...])).astype(a_ref.dtype)
    if zero_state:
        cf_ref[0, :, 0] = cs_ref[...]
        nf_ref[0, :, 0] = ns_ref[...]
        mf_ref[0, :, 0] = m_ref[...]


def mlstm_scan(proj, gates, b_gate, g_head, *, n_batch, seq, row0, state=None, into=None):
    nh, dqk, dv = MLSTM_HEADS, MLSTM_DQK, MLSTM_DV
    rb0 = row0 // seq
    zero_state = state is None
    in_specs = [pl.BlockSpec((seq, dqk), lambda b, h: (rb0 + b, h)),
                pl.BlockSpec((seq, dqk), lambda b, h: (rb0 + b, nh + h)),
                pl.BlockSpec((seq, dv), lambda b, h: (rb0 + b, nh + h)),
                pl.BlockSpec((seq, dv), lambda b, h: (rb0 + b, 2 * nh + h)),
                pl.BlockSpec((seq, LANE), lambda b, h: (rb0 + b, 0)),
                pl.BlockSpec((1, LANE), lambda b, h: (0, 0)),
                pl.BlockSpec((1, dv), lambda b, h: (0, h))]
    bg = jnp.zeros((1, LANE), F32).at[0, :4 * nh].set(b_gate)
    args = [proj, proj, proj, proj, gates, bg, g_head.reshape(1, nh * dv)]
    out_specs = [pl.BlockSpec((seq, dv), lambda b, h: (rb0 + b, h))]
    out_shape = [jax.ShapeDtypeStruct((N_TOK, nh * dv), BF16)]
    c_spec = pl.BlockSpec((1, 2, 1, dqk, dv), lambda b, h: (b, 0, h, 0, 0))
    n_spec = pl.BlockSpec((1, 2, 1, 1, dqk), lambda b, h: (b, 0, h, 0, 0))
    aliases = {}
    if zero_state:
        out_specs += [c_spec, n_spec, n_spec]
        out_shape += [jax.ShapeDtypeStruct((n_batch, 2, nh, dqk, dv), F32),
                      jax.ShapeDtypeStruct((n_batch, 2, nh, 1, dqk), F32),
                      jax.ShapeDtypeStruct((n_batch, 2, nh, 1, LANE), F32)]
    else:
        c0, n0, m0 = state
        in_specs = [pl.BlockSpec(memory_space=pl.ANY)] + in_specs + [c_spec, n_spec, n_spec]
        args = [into] + args + [c0, n0.reshape(n_batch, 2, nh, 1, dqk),
                                jnp.broadcast_to(m0[..., None, None], (n_batch, 2, nh, 1, LANE))]
        aliases = {0: 0}
    return pl.pallas_call(
        functools.partial(_mlstm_kernel, seq=seq, zero_state=zero_state),
        grid=(n_batch, nh),
        in_specs=in_specs,
        out_specs=out_specs,
        out_shape=out_shape,
        input_output_aliases=aliases,
        scratch_shapes=[pltpu.VMEM((seq, dv), F32), pltpu.VMEM((seq, dv), F32),
                        pltpu.VMEM((2, dqk, dv), F32), pltpu.VMEM((2, 1, dqk), F32), pltpu.VMEM((2, 1, LANE), F32)],
        compiler_params=_params(("parallel", "parallel")),
        name="mlstm_scan_ctx" if zero_state else "mlstm_scan_lat",
    )(*args)


def _conv_silu_kernel(*refs, seq, aliased):
    if aliased:
        refs = refs[1:]
    x_ref, w_ref, b_ref, o_ref = refs
    x = x_ref[...]
    t = _iota((seq, 1), 0)
    half = SSD_CONV // 2
    acc = b_ref[...] + w_ref[half:half + 1, :] * x
    for k in range(SSD_CONV):
        off = k - half
        if off == 0:
            continue
        shifted = pltpu.roll(x, (-off) % seq, axis=0)
        valid = (t + off >= 0) & (t + off < seq)
        acc = acc + w_ref[k:k + 1, :] * jnp.where(valid, shifted, 0.0)
    o_ref[...] = acc * jax.nn.sigmoid(acc)


def conv_silu(proj, conv_w, conv_b, *, n_batch, seq, row0, col0, into=None, tc=512):
    ch = conv_w.shape[1]
    rb0 = row0 // seq
    aliased = into is not None
    in_specs = [pl.BlockSpec((seq, tc), lambda b, j: (rb0 + b, col0 // tc + j)),
                pl.BlockSpec((SSD_CONV, tc), lambda b, j: (0, j)),
                pl.BlockSpec((1, tc), lambda b, j: (0, j))]
    args = [proj, conv_w, conv_b.reshape(1, ch)]
    if aliased:
        in_specs = [pl.BlockSpec(memory_space=pl.ANY)] + in_specs
        args = [into] + args
    return pl.pallas_call(
        functools.partial(_conv_silu_kernel, seq=seq, aliased=aliased),
        grid=(n_batch, ch // tc),
        in_specs=in_specs,
        out_specs=pl.BlockSpec((seq, tc), lambda b, j: (rb0 + b, j)),
        out_shape=jax.ShapeDtypeStruct((N_TOK, ch), F32),
        input_output_aliases={0: 0} if aliased else {},
        compiler_params=_params(("parallel", "parallel")),
        name="ssd_conv",
    )(*args)


def _ssd_kernel(*refs, seq, zero_state):
    if zero_state:
        (xs_ref, bm_ref, cm_ref, dt_ref, z_ref, par_ref, dsk_ref,
         y_ref, ssq_ref, sf_ref, yb_ref, st_ref) = refs
    else:
        (into_y, into_q, xs_ref, bm_ref, cm_ref, dt_ref, z_ref, par_ref, dsk_ref, s0_ref,
         y_ref, ssq_ref, yb_ref, st_ref) = refs
        del into_y, into_q
    L = SSD_CHUNK
    nc = seq // L
    R, P = SSD_R, SSD_HEADDIM
    grp = pl.program_id(1)

    if zero_state:
        st_ref[...] = jnp.zeros_like(st_ref)
    else:
        st_ref[...] = s0_ref[0]

    rr = _iota((LANE, LANE), 0)
    cc = _iota((LANE, LANE), 1)
    sel = jnp.where((cc < 2 * R) & (rr == (cc // R) * SSD_HEADS + grp * R + cc % R), 1.0, 0.0).astype(F32)
    lane = _iota((L, LANE), 1)
    ll = _iota((L, L), 0)
    ss = _iota((L, L), 1)
    lower = (ss <= ll)
    upper = (ss >= ll)
    lowerf = lower.astype(F32)
    dt_bias = par_ref[0, 0:1, :]
    a_neg = -jnp.exp(par_ref[0, 1:2, :])

    def direction(d, c0, out_ref):
        dtc = _softplus(jnp.dot(dt_ref[pl.ds(c0, L), :], sel, preferred_element_type=F32, precision=HIGHEST)
                        + dt_bias)
        dac = dtc * a_neg
        pre = jnp.dot(lowerf, dac, preferred_element_type=F32, precision=HIGHEST)
        tot = pre[L - 1:L, :]
        acum = pre if d == 0 else tot - pre + dac
        acum_t = acum.T
        bmc = bm_ref[pl.ds(c0, L), :].astype(BF16)
        cmc = cm_ref[pl.ds(c0, L), :].astype(BF16)
        cb = lax.dot_general(cmc, bmc, NT, preferred_element_type=F32)
        mask = lower if d == 0 else upper
        for r in range(R):
            u = d * R + r
            col = acum[:, u:u + 1]
            row = acum_t[u:u + 1, :]
            lmat = jnp.exp(jnp.where(mask, col - row, -jnp.inf))
            last = tot[:, u:u + 1]
            xr = xs_ref[pl.ds(c0, L), r * P:(r + 1) * P] * dtc[:, u:u + 1]
            y_diag = jnp.dot((cb * lmat).astype(BF16), xr.astype(BF16), preferred_element_type=F32)
            start = st_ref[d, r]
            y_off = lax.dot_general(cmc, start.astype(BF16), NT, preferred_element_type=F32) * jnp.exp(col)
            decay = jnp.exp(last - col)
            new = lax.dot_general((xr * decay).astype(BF16), bmc, TN, preferred_element_type=F32)
            st_ref[d, r] = jnp.exp(last) * start + new
            out_ref[pl.ds(c0, L), r * P:(r + 1) * P] = y_diag + y_off

    def body(c, carry):
        cf = pl.multiple_of(c * L, L)
        cbk = pl.multiple_of((nc - 1 - c) * L, L)
        direction(0, cf, y_ref)
        direction(1, cbk, yb_ref)
        return carry

    lax.fori_loop(0, nc, body, 0)
    z = z_ref[...]
    y = (y_ref[...] + yb_ref[...] + dsk_ref[...] * xs_ref[...]) * (z * jax.nn.sigmoid(z))
    y_ref[...] = y
    part = jnp.broadcast_to(jnp.sum(y * y, axis=-1, keepdims=True), ssq_ref.shape)

    @pl.when(grp == 0)
    def _():
        ssq_ref[...] = part

    @pl.when(grp > 0)
    def _():
        ssq_ref[...] += part

    if zero_state:
        sf_ref[0] = st_ref[...]


def ssd_scan(proj, xbc, dt_bias, a_log, d_skip, *, n_batch, seq, row0, state=None, into=None):
    di, gw, ng, r = SSD_D_INNER, SSD_GW, SSD_GROUPS, SSD_R
    rb0 = row0 // seq
    zero_state = state is None
    dt_blk = (2 * di + 2 * ng * SSD_STATE) // LANE
    pack = lambda p: p.reshape(2, ng, r).transpose(1, 0, 2).reshape(ng, 2 * r)
    par = jnp.zeros((ng, 8, LANE), F32).at[:, 0, :2 * r].set(pack(dt_bias)).at[:, 1, :2 * r].set(pack(a_log))
    dsk = jnp.repeat(d_skip, SSD_HEADDIM).reshape(1, di)
    in_specs = [pl.BlockSpec((seq, gw), lambda b, g: (rb0 + b, g)),
                pl.BlockSpec((seq, SSD_STATE), lambda b, g: (rb0 + b, di // SSD_STATE + g)),
                pl.BlockSpec((seq, SSD_STATE), lambda b, g: (rb0 + b, di // SSD_STATE + ng + g)),
                pl.BlockSpec((seq, LANE), lambda b, g: (rb0 + b, dt_blk)),
                pl.BlockSpec((seq, gw), lambda b, g: (rb0 + b, g)),
                pl.BlockSpec((1, 8, LANE), lambda b, g: (g, 0, 0)),
                pl.BlockSpec((1, gw), lambda b, g: (0, g))]
    args = [xbc, xbc, xbc, proj, proj, par, dsk]
    out_specs = [pl.BlockSpec((seq, gw), lambda b, g: (rb0 + b, g)),
                 pl.BlockSpec((seq, LANE), lambda b, g: (rb0 + b, 0))]
    out_shape = [jax.ShapeDtypeStruct((N_TOK, di), F32), jax.ShapeDtypeStruct((N_TOK, LANE), F32)]
    s_spec = pl.BlockSpec((1, 2, r, SSD_HEADDIM, SSD_STATE), lambda b, g: (b, 0, g, 0, 0))
    aliases = {}
    if zero_state:
        out_specs.append(s_spec)
        out_shape.append(jax.ShapeDtypeStruct((n_batch, 2, SSD_HEADS, SSD_HEADDIM, SSD_STATE), F32))
    else:
        in_specs = [pl.BlockSpec(memory_space=pl.ANY)] * 2 + in_specs + [s_spec]
        args = list(into) + args + [state]
        aliases = {0: 0, 1: 1}
    return pl.pallas_call(
        functools.partial(_ssd_kernel, seq=seq, zero_state=zero_state),
        grid=(n_batch, ng),
        in_specs=in_specs,
        out_specs=out_specs,
        out_shape=out_shape,
        input_output_aliases=aliases,
        scratch_shapes=[pltpu.VMEM((seq, gw), F32), pltpu.VMEM((2, r, SSD_HEADDIM, SSD_STATE), F32)],
        compiler_params=_params(("parallel", "arbitrary")),
        name="ssd_scan_ctx" if zero_state else "ssd_scan_lat",
    )(*args)


def kernel(x_prompt, x_sample, state_mlstm_c, state_mlstm_n, state_mlstm_m, cache_nat_k, cache_nat_v, cache_gqa_k, cache_gqa_v, state_ssd, c, c_ctx, w_mod, b_mod, norm_g, mlstm_w_in, mlstm_b_gate, mlstm_g_head, mlstm_w_out, nat_w_qkv, nat_rpb, nat_w_out, gqa_w_qkv, gqa_q_g, gqa_k_g, gqa_w_out, ssd_w_in, ssd_conv_w, ssd_conv_b, ssd_dt_bias, ssd_a_log, ssd_d, ssd_g_norm, ssd_w_out, ffn_w_in, ffn_w_out, moe_w_router, moe_b_router, moe_w_in, moe_w_out):
    d = D_MODEL
    h = jnp.concatenate([x_prompt.reshape(N_PROMPT, d), x_sample.reshape(N_LATENT, d)], axis=0)
    cvecs = jnp.zeros((8, d), F32).at[0].set(c_ctx).at[1:1 + DEC_BATCH].set(c)
    mod = modulation_all(cvecs, w_mod, b_mod)[:, :N_GROUPS].reshape(DEPTH, N_GROUPS, 6, 1, d)
    outs = {}
    for layer in range(DEPTH):
        kind, j = layer % N_MIXERS, layer // N_MIXERS
        sh1, sc1, g1, sh2, sc2, g2 = [mod[layer, :, i] for i in range(6)]
        ng = norm_g[layer]
        if kind == 0:
            n_main = 2 * MLSTM_HEADS * MLSTM_DQK + 2 * MLSTM_HEADS * MLSTM_DV
            proj = norm_linear(h, ng[0], sc1, sh1, mlstm_w_in, j, n_out=n_main, name="mlstm_in")
            w_gates = jnp.zeros((1, d, LANE), F32).at[0, :, :4 * MLSTM_HEADS].set(mlstm_w_in[j, :, n_main:])
            gates = norm_linear(h, ng[0], sc1, sh1, w_gates, 0, tn=LANE, name="mlstm_gates")
            a, st_c, st_n, st_m = mlstm_scan(proj, gates, mlstm_b_gate[j], mlstm_g_head[j],
                                             n_batch=BATCH, seq=SEQ, row0=0)
            a = mlstm_scan(proj, gates, mlstm_b_gate[j], mlstm_g_head[j], n_batch=DEC_BATCH, seq=DEC_SEQ,
                           row0=N_PROMPT, state=(state_mlstm_c[:, j], state_mlstm_n[:, j], state_mlstm_m[:, j]),
                           into=a)[0]
            outs['mc'] = st_c[:, None]
            outs['mn'] = st_n[:, None, :, :, 0]
            outs['mm'] = st_m[:, None, :, :, 0, 0]
            h = linear_out(a, mlstm_w_out, j, h, ng[1], g1, name="mlstm_out")
        elif kind == 1:
            nh = NAT_HEADS
            hw = nh * NAT_HD
            qkv = norm_linear(h, ng[0], sc1, sh1, nat_w_qkv, j, name="nat_qkv")
            a = attention(qkv, qkv, qkv, n_batch=BATCH, n_heads=nh, nq=SEQ, nk=SEQ, tq=SEQ, row0=0,
                          q_col=lambda hh: hh, k_col=lambda hh: nh + hh, v_col=lambda hh: 2 * nh + hh,
                          name="nat_ctx_attn")
            a = nat_latent(qkv, cache_nat_k[:, j].reshape(DEC_BATCH, PAST_LEN, hw),
                           cache_nat_v[:, j].reshape(DEC_BATCH, PAST_LEN, hw), nat_bias_slabs(nat_rpb[j]), a)
            outs['nk'] = qkv[:N_PROMPT, hw:2 * hw].reshape(BATCH, 1, SEQ, nh, NAT_HD)
            outs['nv'] = qkv[:N_PROMPT, 2 * hw:].reshape(BATCH, 1, SEQ, nh, NAT_HD)
            h = linear_out(a, nat_w_out, j, h, ng[1], g1, name="nat_out")
        elif kind == 2:
            nqk = GQA_HEADS + GQA_KV_HEADS
            grp = GQA_HEADS // GQA_KV_HEADS
            kw = GQA_KV_HEADS * GQA_HD
            qkv = norm_linear(h, ng[0], sc1, sh1, gqa_w_qkv, j, name="gqa_qkv")
            gains = jnp.concatenate([jnp.tile(gqa_q_g[j], GQA_HEADS), jnp.tile(gqa_k_g[j], GQA_KV_HEADS)])
            cos, sa, sb = rope_tables()
            qk = qk_norm_rope(qkv, gains.reshape(1, -1), cos, sa, sb, n_heads=nqk)
            cols = dict(q_col=lambda hh: hh, k_col=lambda hh: GQA_HEADS + hh // grp,
                        v_col=lambda hh: nqk + hh // grp)
            a = attention(qk, qk, qkv, n_batch=BATCH, n_heads=GQA_HEADS, nq=SEQ, nk=SEQ, tq=SEQ, row0=0,
                          name="gqa_ctx_attn", **cols)
            a = attention(qk, qk, qkv, n_batch=DEC_BATCH, n_heads=GQA_HEADS, nq=DEC_SEQ, nk=DEC_SEQ, tq=256,
                          row0=N_PROMPT, k_cache=cache_gqa_k[:, j].reshape(DEC_BATCH, PAST_LEN, kw),
                          v_cache=cache_gqa_v[:, j].reshape(DEC_BATCH, PAST_LEN, kw),
                          kv_of_head=lambda hh: hh // grp, into=a, name="gqa_lat_attn", **cols)
            outs['gk'] = qk[:N_PROMPT, GQA_HEADS * GQA_HD:].reshape(BATCH, 1, SEQ, GQA_KV_HEADS, GQA_HD)
            outs['gv'] = qkv[:N_PROMPT, GQA_HEADS * GQA_HD + kw:].reshape(BATCH, 1, SEQ, GQA_KV_HEADS, GQA_HD)
            h = linear_out(a, gqa_w_out, j, h, ng[1], g1, name="gqa_out")
        else:
            proj = norm_linear(h, ng[0], sc1, sh1, ssd_w_in, j, tn=384, name="ssd_in")
            conv = dict(col0=SSD_D_INNER)
            xbc = conv_silu(proj, ssd_conv_w[j], ssd_conv_b[j], n_batch=BATCH, seq=SEQ, row0=0, **conv)
            xbc = conv_silu(proj, ssd_conv_w[j], ssd_conv_b[j], n_batch=DEC_BATCH, seq=DEC_SEQ, row0=N_PROMPT,
                            into=xbc, **conv)
            y, ssq, st_s = ssd_scan(proj, xbc, ssd_dt_bias[j], ssd_a_log[j], ssd_d[j],
                                    n_batch=BATCH, seq=SEQ, row0=0)
            y, ssq = ssd_scan(proj, xbc, ssd_dt_bias[j], ssd_a_log[j], ssd_d[j], n_batch=DEC_BATCH, seq=DEC_SEQ,
                              row0=N_PROMPT, state=state_ssd[:, j], into=(y, ssq))
            outs['ss'] = st_s[:, None]
            h = linear_out(y, ssd_w_out, j, h, ng[1], g1, ssq=ssq, gn=ssd_g_norm[j], name="ssd_out")
        e = layer // 2
        if layer % 2 == 0:
            act = norm_swiglu_in(h, ng[2], sc2, sh2, ffn_w_in, e)
            h = linear_out(act, ffn_w_out, e, h, ng[3], g2, name="ffn_out")
        else:
            h = moe_layer(h, ng[2], sc2, sh2, moe_w_router[e], moe_b_router[e], moe_w_in, moe_w_out, e,
                          ng[3], g2)
    y_prompt = h[:N_PROMPT].reshape(BATCH, SEQ, d)
    y_sample = h[N_PROMPT:].reshape(DEC_BATCH, DEC_SEQ, d)
    return (y_prompt, y_sample, outs['mc'], outs['mn'], outs['mm'], outs['nk'], outs['nv'],
            outs['gk'], outs['gv'], outs['ss'])
```

```python
import functools

import jax
import jax.numpy as jnp
from jax import lax
from jax.experimental import pallas as pl
from jax.experimental.pallas import tpu as pltpu

D_MODEL = 2048
BATCH = 16
SEQ = 256
DEPTH = 4
DEC_BATCH = 2
DEC_SEQ = 2048
PAST_LEN = 512
GRID_W = 64
N_MIXERS = 4

MLSTM_HEADS = 8
MLSTM_DV = D_MODEL // MLSTM_HEADS
MLSTM_DQK = MLSTM_DV // 2
MLSTM_CHUNK = 64

NAT_HEADS = 16
NAT_HD = D_MODEL // NAT_HEADS
NAT_KH = 8
NAT_KW = 16

GQA_HEADS = 16
GQA_KV_HEADS = 4
GQA_HD = D_MODEL // GQA_HEADS
ROPE_THETA = 10000.0

SSD_D_INNER = 2 * D_MODEL
SSD_HEADDIM = 64
SSD_HEADS = SSD_D_INNER // SSD_HEADDIM
SSD_GROUPS = 8
SSD_STATE = 128
SSD_CONV = 5
SSD_CHUNK = 128
SSD_R = SSD_HEADS // SSD_GROUPS
SSD_GW = SSD_R * SSD_HEADDIM

D_FF = 7 * D_MODEL // 2
N_EXPERTS = 8
TOP_K = 2

EPS = 1e-6
NEG = -1e30

N_PROMPT = BATCH * SEQ
N_LATENT = DEC_BATCH * DEC_SEQ
N_TOK = N_PROMPT + N_LATENT
N_GROUPS = 1 + DEC_BATCH
LANE = 128
VMEM_LIMIT = 56 * 1024 * 1024

F32 = jnp.float32
BF16 = jnp.bfloat16
HIGHEST = lax.Precision.HIGHEST
NT = (((1,), (1,)), ((), ()))
TN = (((0,), (0,)), ((), ()))


def _params(sem, vmem=VMEM_LIMIT):
    return pltpu.CompilerParams(dimension_semantics=sem, vmem_limit_bytes=vmem)


def _group_of_tile(i, tm):
    return jnp.maximum(i * tm // DEC_SEQ - (N_PROMPT // DEC_SEQ - 1), 0)


def _rms(x, g):
    return x * lax.rsqrt(jnp.mean(x * x, axis=-1, keepdims=True) + EPS) * g


def _softplus(x):
    return jnp.maximum(x, 0.0) + jnp.log(1.0 + jnp.exp(-jnp.abs(x)))


def _iota(shape, dim):
    return lax.broadcasted_iota(jnp.int32, shape, dim)


def _mod_kernel(c_ref, w_ref, b_ref, o_ref):
    c = c_ref[...]
    s = (c * jax.nn.sigmoid(c)).astype(BF16)
    o_ref[0] = jnp.dot(s, w_ref[0].astype(BF16), preferred_element_type=F32) + b_ref[0]


def modulation_all(cvecs, w_mod, b_mod):
    tn = 1024
    n = w_mod.shape[-1]
    return pl.pallas_call(
        _mod_kernel,
        grid=(DEPTH, n // tn),
        in_specs=[pl.BlockSpec((8, D_MODEL), lambda l, j: (0, 0)),
                  pl.BlockSpec((1, D_MODEL, tn), lambda l, j: (l, 0, j)),
                  pl.BlockSpec((1, 1, tn), lambda l, j: (l, 0, j))],
        out_specs=pl.BlockSpec((1, 8, tn), lambda l, j: (l, 0, j)),
        out_shape=jax.ShapeDtypeStruct((DEPTH, 8, n), F32),
        compiler_params=_params(("parallel", "parallel")),
        name="modulation",
    )(cvecs, w_mod, b_mod.reshape(DEPTH, 1, n))


def _norm_linear_kernel(x_ref, g_ref, sc_ref, sh_ref, w_ref, o_ref, u_ref):
    @pl.when(pl.program_id(1) == 0)
    def _():
        u = _rms(x_ref[...], g_ref[...]) * (1.0 + sc_ref[0]) + sh_ref[0]
        u_ref[...] = u.astype(BF16)

    o_ref[...] = jnp.dot(u_ref[...], w_ref[0].astype(BF16), preferred_element_type=F32).astype(o_ref.dtype)


def norm_linear(h, g, scale, shift, w, wl, *, n_out=None, tm=1024, tn=512, out_dtype=F32, name="norm_linear"):
    m, d = h.shape
    n_out = w.shape[2] if n_out is None else n_out
    return pl.pallas_call(
        _norm_linear_kernel,
        grid=(m // tm, pl.cdiv(n_out, tn)),
        in_specs=[pl.BlockSpec((tm, d), lambda i, j: (i, 0)),
                  pl.BlockSpec((1, d), lambda i, j: (0, 0)),
                  pl.BlockSpec((1, 1, d), lambda i, j: (_group_of_tile(i, tm), 0, 0)),
                  pl.BlockSpec((1, 1, d), lambda i, j: (_group_of_tile(i, tm), 0, 0)),
                  pl.BlockSpec((1, d, tn), lambda i, j: (wl, 0, j))],
        out_specs=pl.BlockSpec((tm, tn), lambda i, j: (i, j)),
        out_shape=jax.ShapeDtypeStruct((m, n_out), out_dtype),
        scratch_shapes=[pltpu.VMEM((tm, d), BF16)],
        compiler_params=_params(("parallel", "arbitrary")),
        name=name,
    )(h, g.reshape(1, d), scale, shift, w)


def _norm_swiglu_kernel(x_ref, g_ref, sc_ref, sh_ref, wg_ref, wu_ref, o_ref, u_ref):
    @pl.when(pl.program_id(1) == 0)
    def _():
        u = _rms(x_ref[...], g_ref[...]) * (1.0 + sc_ref[0]) + sh_ref[0]
        u_ref[...] = u.astype(BF16)

    u = u_ref[...]
    a = jnp.dot(u, wg_ref[0].astype(BF16), preferred_element_type=F32)
    b = jnp.dot(u, wu_ref[0].astype(BF16), preferred_element_type=F32)
    o_ref[...] = (a * jax.nn.sigmoid(a) * b).astype(o_ref.dtype)


def norm_swiglu_in(h, g, scale, shift, w_in, wl, *, tm=1024, tn=256):
    m, d = h.shape
    f = w_in.shape[2] // 2
    nj = f // tn
    return pl.pallas_call(
        _norm_swiglu_kernel,
        grid=(m // tm, nj),
        in_specs=[pl.BlockSpec((tm, d), lambda i, j: (i, 0)),
                  pl.BlockSpec((1, d), lambda i, j: (0, 0)),
                  pl.BlockSpec((1, 1, d), lambda i, j: (_group_of_tile(i, tm), 0, 0)),
                  pl.BlockSpec((1, 1, d), lambda i, j: (_group_of_tile(i, tm), 0, 0)),
                  pl.BlockSpec((1, d, tn), lambda i, j: (wl, 0, j)),
                  pl.BlockSpec((1, d, tn), lambda i, j: (wl, 0, j + nj))],
        out_specs=pl.BlockSpec((tm, tn), lambda i, j: (i, j)),
        out_shape=jax.ShapeDtypeStruct((m, f), BF16),
        scratch_shapes=[pltpu.VMEM((tm, d), BF16)],
        compiler_params=_params(("parallel", "arbitrary")),
        name="norm_swiglu_in",
    )(h, g.reshape(1, d), scale, shift, w_in, w_in)


def _linear_out_kernel(*refs, prescale, kdim):
    if prescale:
        a_ref, w_ref, h_ref, g_ref, gate_ref, ssq_ref, gn_ref, o_ref, acc_ref = refs
    else:
        a_ref, w_ref, h_ref, g_ref, gate_ref, o_ref, acc_ref = refs
    k = pl.program_id(1)

    @pl.when(k == 0)
    def _():
        acc_ref[...] = jnp.zeros_like(acc_ref)

    a = a_ref[...]
    if prescale:
        a = a * lax.rsqrt(ssq_ref[:, :1] * (1.0 / kdim) + EPS) * gn_ref[...]
    acc_ref[...] += jnp.dot(a.astype(BF16), w_ref[0].astype(BF16), preferred_element_type=F32)

    @pl.when(k == pl.num_programs(1) - 1)
    def _():
        o_ref[...] = h_ref[...] + gate_ref[0] * _rms(acc_ref[...], g_ref[...])


def linear_out(a, w, wl, h, g, gate, *, ssq=None, gn=None, tm=512, tk=512, name="linear_out"):
    m, kdim = a.shape
    d = w.shape[2]
    prescale = ssq is not None
    in_specs = [pl.BlockSpec((tm, tk), lambda i, k: (i, k)),
                pl.BlockSpec((1, tk, d), lambda i, k: (wl, k, 0)),
                pl.BlockSpec((tm, d), lambda i, k: (i, 0)),
                pl.BlockSpec((1, d), lambda i, k: (0, 0)),
                pl.BlockSpec((1, 1, d), lambda i, k: (_group_of_tile(i, tm), 0, 0))]
    args = [a, w, h, g.reshape(1, d), gate]
    if prescale:
        in_specs += [pl.BlockSpec((tm, LANE), lambda i, k: (i, 0)),
                     pl.BlockSpec((1, tk), lambda i, k: (0, k))]
        args += [ssq, gn.reshape(1, kdim)]
    return pl.pallas_call(
        functools.partial(_linear_out_kernel, prescale=prescale, kdim=kdim),
        grid=(m // tm, kdim // tk),
        in_specs=in_specs,
        out_specs=pl.BlockSpec((tm, d), lambda i, k: (i, 0)),
        out_shape=jax.ShapeDtypeStruct((m, d), F32),
        scratch_shapes=[pltpu.VMEM((tm, d), F32)],
        compiler_params=_params(("parallel", "arbitrary")),
        name=name,
    )(*args)


def _norm_router_kernel(x_ref, g_ref, sc_ref, sh_ref, wr_ref, br_ref, u_ref, rt_ref):
    u = _rms(x_ref[...], g_ref[...]) * (1.0 + sc_ref[0]) + sh_ref[0]
    u_ref[...] = u
    lg = jnp.dot(u, wr_ref[...], preferred_element_type=F32, precision=HIGHEST) + br_ref[...]
    lane = _iota(lg.shape, 1)
    lanef = lane.astype(F32)
    lg = jnp.where(lane < N_EXPERTS, lg, -jnp.inf)
    m1 = jnp.max(lg, axis=-1, keepdims=True)
    i1 = jnp.min(jnp.where(lg == m1, lanef, float(LANE)), axis=-1, keepdims=True)
    lg2 = jnp.where(lanef == i1, -jnp.inf, lg)
    m2 = jnp.max(lg2, axis=-1, keepdims=True)
    i2 = jnp.min(jnp.where(lg2 == m2, lanef, float(LANE)), axis=-1, keepdims=True)
    e = jnp.exp(m2 - m1)
    w1 = 1.0 / (1.0 + e)
    w2 = e / (1.0 + e)
    rt_ref[...] = jnp.where(lane == 0, i1, jnp.where(lane == 1, i2, jnp.where(lane == 2, w1,
                            jnp.where(lane == 3, w2, 0.0))))


def norm_router(h, g, scale, shift, w_router, b_router, *, tm=512):
    m, d = h.shape
    wr = jnp.zeros((d, LANE), F32).at[:, :N_EXPERTS].set(w_router)
    br = jnp.zeros((1, LANE), F32).at[0, :N_EXPERTS].set(b_router)
    return pl.pallas_call(
        _norm_router_kernel,
        grid=(m // tm,),
        in_specs=[pl.BlockSpec((tm, d), lambda i: (i, 0)),
                  pl.BlockSpec((1, d), lambda i: (0, 0)),
                  pl.BlockSpec((1, 1, d), lambda i: (_group_of_tile(i, tm), 0, 0)),
                  pl.BlockSpec((1, 1, d), lambda i: (_group_of_tile(i, tm), 0, 0)),
                  pl.BlockSpec((d, LANE), lambda i: (0, 0)),
                  pl.BlockSpec((1, LANE), lambda i: (0, 0))],
        out_specs=[pl.BlockSpec((tm, d), lambda i: (i, 0)),
                   pl.BlockSpec((tm, LANE), lambda i: (i, 0))],
        out_shape=[jax.ShapeDtypeStruct((m, d), F32), jax.ShapeDtypeStruct((m, LANE), F32)],
        compiler_params=_params(("parallel",)),
        name="norm_router",
    )(h, g.reshape(1, d), scale, shift, wr, br)


MOE_TM = 512


def _row_copy(src_hbm, idx, dst, r, sem):
    return pltpu.make_async_copy(src_hbm.at[pl.ds(idx, 1), :], dst.at[pl.ds(r, 1), :], sem)


def _gather_all(idx_ref, base, src_hbm, dst, sem, n):
    def issue(r, c):
        _row_copy(src_hbm, idx_ref[base + r], dst, r, sem).start()
        return c

    lax.fori_loop(0, n, issue, 0, unroll=8)

    def drain(r, c):
        _row_copy(src_hbm, 0, dst, r, sem).wait()
        return c

    lax.fori_loop(0, n, drain, 0, unroll=8)


def _moe_gather_kernel(tok_ref, nu_ref, u_hbm, o_ref, buf, sem):
    @pl.when(pl.program_id(0) < nu_ref[0])
    def _():
        _gather_all(tok_ref, pl.program_id(0) * MOE_TM, u_hbm, buf, sem, MOE_TM)
        o_ref[...] = buf[...].astype(o_ref.dtype)

    @pl.when(pl.program_id(0) >= nu_ref[0])
    def _():
        o_ref[...] = jnp.zeros_like(o_ref)


def moe_gather(u, row_tok, n_used):
    r = row_tok.shape[0]
    d = u.shape[1]
    return pl.pallas_call(
        _moe_gather_kernel,
        grid_spec=pltpu.PrefetchScalarGridSpec(
            num_scalar_prefetch=2,
            grid=(r // MOE_TM,),
            in_specs=[pl.BlockSpec(memory_space=pl.ANY)],
            out_specs=pl.BlockSpec((MOE_TM, d), lambda b, tok, nu: (b, 0)),
            scratch_shapes=[pltpu.VMEM((MOE_TM, d), F32), pltpu.SemaphoreType.DMA(())],
        ),
        out_shape=jax.ShapeDtypeStruct((r, d), BF16),
        compiler_params=_params(("arbitrary",)),
        name="moe_gather",
    )(row_tok, n_used, u)


def _moe_in_kernel(be_ref, nu_ref, x_ref, wg_ref, wu_ref, o_ref):
    @pl.when(pl.program_id(1) < nu_ref[0])
    def _():
        x = x_ref[...]
        a = jnp.dot(x, wg_ref[0, 0].astype(BF16), preferred_element_type=F32)
        b = jnp.dot(x, wu_ref[0, 0].astype(BF16), preferred_element_type=F32)
        o_ref[...] = (a * jax.nn.sigmoid(a) * b).astype(o_ref.dtype)

    @pl.when(pl.program_id(1) >= nu_ref[0])
    def _():
        o_ref[...] = jnp.zeros_like(o_ref)


def _used_block(b, nu):
    return jnp.minimum(b, nu[0] - 1)


def moe_in(x_rows, blk_expert, n_used, w_in, wl, *, tn=512):
    r, d = x_rows.shape
    f = w_in.shape[3] // 2
    nj = f // tn
    nb = r // MOE_TM
    return pl.pallas_call(
        _moe_in_kernel,
        grid_spec=pltpu.PrefetchScalarGridSpec(
            num_scalar_prefetch=2,
            grid=(nj, nb),
            in_specs=[pl.BlockSpec((MOE_TM, d), lambda j, b, be, nu: (_used_block(b, nu), 0)),
                      pl.BlockSpec((1, 1, d, tn), lambda j, b, be, nu: (wl, be[_used_block(b, nu)], 0, j)),
                      pl.BlockSpec((1, 1, d, tn), lambda j, b, be, nu: (wl, be[_used_block(b, nu)], 0, j + nj))],
            out_specs=pl.BlockSpec((MOE_TM, tn), lambda j, b, be, nu: (b, j)),
        ),
        out_shape=jax.ShapeDtypeStruct((r, f), BF16),
        compiler_params=_params(("arbitrary", "arbitrary")),
        name="moe_in",
    )(blk_expert, n_used, x_rows, w_in, w_in)


def _moe_out_kernel(be_ref, nu_ref, a_ref, w_ref, rw_ref, o_ref):
    @pl.when(pl.program_id(1) < nu_ref[0])
    def _():
        y = jnp.dot(a_ref[...], w_ref[0, 0].astype(BF16), preferred_element_type=F32)
        o_ref[...] = y * rw_ref[...]

    @pl.when(pl.program_id(1) >= nu_ref[0])
    def _():
        o_ref[...] = jnp.zeros_like(o_ref)


def moe_out(act_rows, blk_expert, n_used, w_out, wl, row_w, *, tn=256):
    r, f = act_rows.shape
    d = w_out.shape[3]
    nb = r // MOE_TM
    return pl.pallas_call(
        _moe_out_kernel,
        grid_spec=pltpu.PrefetchScalarGridSpec(
            num_scalar_prefetch=2,
            grid=(d // tn, nb),
            in_specs=[pl.BlockSpec((MOE_TM, f), lambda j, b, be, nu: (_used_block(b, nu), 0)),
                      pl.BlockSpec((1, 1, f, tn), lambda j, b, be, nu: (wl, be[_used_block(b, nu)], 0, j)),
                      pl.BlockSpec((MOE_TM, 1), lambda j, b, be, nu: (_used_block(b, nu), 0))],
            out_specs=pl.BlockSpec((MOE_TM, tn), lambda j, b, be, nu: (b, j)),
        ),
        out_shape=jax.ShapeDtypeStruct((r, d), F32),
        compiler_params=_params(("arbitrary", "arbitrary")),
        name="moe_out",
    )(blk_expert, n_used, act_rows, w_out, row_w.reshape(r, 1))


COMBINE_TM = 256


def _moe_combine_kernel(pos_ref, y_hbm, h_ref, g_ref, gate_ref, o_ref, buf, sem):
    n = TOP_K * COMBINE_TM
    _gather_all(pos_ref, pl.program_id(0) * n, y_hbm, buf, sem, n)
    f = buf[:COMBINE_TM, :] + buf[COMBINE_TM:, :]
    o_ref[...] = h_ref[...] + gate_ref[0] * _rms(f, g_ref[...])


def moe_combine(y_rows, pos_blocked, h, g, gate):
    m, d = h.shape
    tm = COMBINE_TM
    return pl.pallas_call(
        _moe_combine_kernel,
        grid_spec=pltpu.PrefetchScalarGridSpec(
            num_scalar_prefetch=1,
            grid=(m // tm,),
            in_specs=[pl.BlockSpec(memory_space=pl.ANY),
                      pl.BlockSpec((tm, d), lambda i, pos: (i, 0)),
                      pl.BlockSpec((1, d), lambda i, pos: (0, 0)),
                      pl.BlockSpec((1, 1, d), lambda i, pos: (_group_of_tile(i, tm), 0, 0))],
            out_specs=pl.BlockSpec((tm, d), lambda i, pos: (i, 0)),
            scratch_shapes=[pltpu.VMEM((TOP_K * tm, d), F32), pltpu.SemaphoreType.DMA(())],
        ),
        out_shape=jax.ShapeDtypeStruct((m, d), F32),
        compiler_params=_params(("arbitrary",)),
        name="moe_combine",
    )(pos_blocked, y_rows, h, g.reshape(1, d), gate)


def moe_layer(h, g, scale, shift, w_router, b_router, w_in, w_out, wl, g_post, gate):
    t = h.shape[0]
    u, route = norm_router(h, g, scale, shift, w_router, b_router)
    expert = route[:, :TOP_K].astype(jnp.int32).reshape(-1)
    weight = route[:, TOP_K:2 * TOP_K].reshape(-1)
    n_assign = t * TOP_K
    onehot = (expert[:, None] == jnp.arange(N_EXPERTS, dtype=jnp.int32)[None, :]).astype(jnp.int32)
    rank = jnp.sum((jnp.cumsum(onehot, axis=0) - onehot) * onehot, axis=1)
    counts = jnp.sum(onehot, axis=0)
    padded = (counts + MOE_TM - 1) // MOE_TM * MOE_TM
    pad_end = jnp.cumsum(padded)
    pad_start = pad_end - padded
    pos = (pad_start[expert] + rank).astype(jnp.int32)
    n_blocks = n_assign // MOE_TM + N_EXPERTS
    n_rows = n_blocks * MOE_TM
    token = jnp.repeat(jnp.arange(t, dtype=jnp.int32), TOP_K)
    row_tok = jnp.zeros((n_rows,), jnp.int32).at[pos].set(token)
    row_w = jnp.zeros((n_rows,), F32).at[pos].set(weight)
    blk_expert = jnp.minimum(
        jnp.searchsorted(pad_end, jnp.arange(n_blocks, dtype=jnp.int32) * MOE_TM, side='right'),
        N_EXPERTS - 1).astype(jnp.int32)
    n_used = (pad_end[-1] // MOE_TM).astype(jnp.int32).reshape(1)
    pos_blocked = pos.reshape(t // COMBINE_TM, COMBINE_TM, TOP_K).transpose(0, 2, 1).reshape(-1)

    x_rows = moe_gather(u, row_tok, n_used)
    act = moe_in(x_rows, blk_expert, n_used, w_in, wl)
    y_rows = moe_out(act, blk_expert, n_used, w_out, wl, row_w)
    return moe_combine(y_rows, pos_blocked, h, g_post, gate)


def _attn_kernel(*refs, scale, two, aliased):
    if aliased:
        refs = refs[1:]
    if two:
        q_ref, k_ref, v_ref, kc_ref, vc_ref, o_ref = refs
    else:
        q_ref, k_ref, v_ref, o_ref = refs
    q = q_ref[...].astype(BF16)
    s = lax.dot_general(q, k_ref[...].astype(BF16), NT, preferred_element_type=F32) * scale
    m = jnp.max(s, axis=-1, keepdims=True)
    if two:
        s2 = lax.dot_general(q, kc_ref[0].astype(BF16), NT, preferred_element_type=F32) * scale
        m = jnp.maximum(m, jnp.max(s2, axis=-1, keepdims=True))
    p = jnp.exp(s - m)
    l = jnp.sum(p, axis=-1, keepdims=True)
    o = jnp.dot(p.astype(BF16), v_ref[...].astype(BF16), preferred_element_type=F32)
    if two:
        p2 = jnp.exp(s2 - m)
        l = l + jnp.sum(p2, axis=-1, keepdims=True)
        o = o + jnp.dot(p2.astype(BF16), vc_ref[0].astype(BF16), preferred_element_type=F32)
    o_ref[...] = (o / l).astype(o_ref.dtype)


def attention(q_arr, k_arr, v_arr, *, n_batch, n_heads, nq, nk, tq, row0, q_col, k_col, v_col,
              k_cache=None, v_cache=None, kv_of_head=lambda h: h, into=None, name="attention"):
    hd = LANE
    nqb = nq // tq
    two = k_cache is not None
    aliased = into is not None
    in_specs = [pl.BlockSpec((tq, hd), lambda b, h, i: (row0 // tq + b * nqb + i, q_col(h))),
                pl.BlockSpec((nk, hd), lambda b, h, i: (row0 // nk + b, k_col(h))),
                pl.BlockSpec((nk, hd), lambda b, h, i: (row0 // nk + b, v_col(h)))]
    args = [q_arr, k_arr, v_arr]
    if two:
        n_past = k_cache.shape[1]
        in_specs += [pl.BlockSpec((1, n_past, hd), lambda b, h, i: (b, 0, kv_of_head(h))),
                     pl.BlockSpec((1, n_past, hd), lambda b, h, i: (b, 0, kv_of_head(h)))]
        args += [k_cache, v_cache]
    if aliased:
        in_specs = [pl.BlockSpec(memory_space=pl.ANY)] + in_specs
        args = [into] + args
    return pl.pallas_call(
        functools.partial(_attn_kernel, scale=hd ** -0.5, two=two, aliased=aliased),
        grid=(n_batch, n_heads, nqb),
        in_specs=in_specs,
        out_specs=pl.BlockSpec((tq, hd), lambda b, h, i: (row0 // tq + b * nqb + i, h)),
        out_shape=jax.ShapeDtypeStruct((N_TOK, n_heads * hd), BF16),
        input_output_aliases={0: 0} if aliased else {},
        compiler_params=_params(("parallel", "parallel", "arbitrary")),
        name=name,
    )(*args)


def _nat_latent_kernel(into_ref, q_ref, k_ref, v_ref, kc_ref, vc_ref, b_ref, o_ref, kb_ref, vb_ref, *, rows):
    del into_ref
    w, kh = GRID_W, NAT_KH
    nwin = kh * w
    half = kh // 2
    scale = NAT_HD ** -0.5
    qi = _iota((w, nwin), 0)
    kk = _iota((w, nwin), 1) % w
    c_start = jnp.clip(qi - NAT_KW // 2, 0, w - NAT_KW)
    col_ok = (kk >= c_start) & (kk < c_start + NAT_KW)
    kb_ref[...] = k_ref[...].astype(BF16)
    vb_ref[...] = v_ref[...].astype(BF16)
    kc = kc_ref[0].astype(BF16)
    vc = vc_ref[0].astype(BF16)

    def one_row(r, start, rel):
        q0 = r * w
        k0 = start * w
        if not isinstance(r, int):
            q0 = pl.multiple_of(q0, w)
            k0 = pl.multiple_of(k0, w)
        q = q_ref[pl.ds(q0, w), :].astype(BF16)
        s1 = lax.dot_general(q, kb_ref[pl.ds(k0, nwin), :], NT, preferred_element_type=F32) * scale + b_ref[0, rel]
        s1 = jnp.where(col_ok, s1, NEG)
        s2 = lax.dot_general(q, kc, NT, preferred_element_type=F32) * scale
        m = jnp.maximum(jnp.max(s1, axis=-1, keepdims=True), jnp.max(s2, axis=-1, keepdims=True))
        p1 = jnp.exp(s1 - m)
        p2 = jnp.exp(s2 - m)
        l = jnp.sum(p1, axis=-1, keepdims=True) + jnp.sum(p2, axis=-1, keepdims=True)
        o = (jnp.dot(p1.astype(BF16), vb_ref[pl.ds(k0, nwin), :], preferred_element_type=F32)
             + jnp.dot(p2.astype(BF16), vc, preferred_element_type=F32))
        o_ref[pl.ds(q0, w), :] = (o / l).astype(o_ref.dtype)

    last_start = rows - kh
    for r in range(half):
        one_row(r, 0, kh - 1 - r)

    def interior(r, c):
        one_row(r, r - half, kh - 1 - half)
        return c

    lax.fori_loop(half, last_start + half + 1, interior, 0)
    for r in range(last_start + half + 1, rows):
        one_row(r, last_start, last_start - r + kh - 1)


def nat_bias_slabs(rpb):
    c = jnp.arange(GRID_W)
    col_off = jnp.clip(c[None, :] - c[:, None], -(NAT_KW - 1), NAT_KW - 1) + NAT_KW - 1
    t = rpb[:, :, col_off]
    rel = jnp.arange(NAT_KH)[:, None] + jnp.arange(NAT_KH)[None, :]
    return t[:, rel].transpose(0, 1, 3, 2, 4).reshape(rpb.shape[0], NAT_KH, GRID_W, NAT_KH * GRID_W)


def nat_latent(qkv, k_cache, v_cache, bias, into):
    nh, hd = NAT_HEADS, NAT_HD
    rows = DEC_SEQ // GRID_W
    rb0 = N_PROMPT // DEC_SEQ
    blk = lambda c0: pl.BlockSpec((DEC_SEQ, hd), lambda b, h: (rb0 + b, c0 + h))
    return pl.pallas_call(
        functools.partial(_nat_latent_kernel, rows=rows),
        grid=(DEC_BATCH, nh),
        in_specs=[pl.BlockSpec(memory_space=pl.ANY), blk(0), blk(nh), blk(2 * nh),
                  pl.BlockSpec((1, PAST_LEN, hd), lambda b, h: (b, 0, h)),
                  pl.BlockSpec((1, PAST_LEN, hd), lambda b, h: (b, 0, h)),
                  pl.BlockSpec((1, NAT_KH, GRID_W, NAT_KH * GRID_W), lambda b, h: (h, 0, 0, 0))],
        out_specs=pl.BlockSpec((DEC_SEQ, hd), lambda b, h: (rb0 + b, h)),
        out_shape=jax.ShapeDtypeStruct((N_TOK, nh * hd), BF16),
        input_output_aliases={0: 0},
        scratch_shapes=[pltpu.VMEM((DEC_SEQ, hd), BF16), pltpu.VMEM((DEC_SEQ, hd), BF16)],
        compiler_params=_params(("parallel", "parallel")),
        name="nat_latent",
    )(into, qkv, qkv, qkv, k_cache, v_cache, bias)


def _qk_norm_rope_kernel(x_ref, g_ref, cos_ref, sa_ref, sb_ref, o_ref, *, n_heads):
    cos, sa, sb = cos_ref[...], sa_ref[...], sb_ref[...]
    quarter = GQA_HD // 4
    for hh in range(n_heads):
        sl = slice(hh * GQA_HD, (hh + 1) * GQA_HD)
        y = _rms(x_ref[:, sl], g_ref[:, sl])
        up = pltpu.roll(y, GQA_HD - quarter, axis=1)
        dn = pltpu.roll(y, quarter, axis=1)
        o_ref[:, sl] = y * cos + up * sa + dn * sb


def qk_norm_rope(qkv, gains, cos, sa, sb, *, n_heads, tm=512):
    m = qkv.shape[0]
    w = n_heads * GQA_HD
    return pl.pallas_call(
        functools.partial(_qk_norm_rope_kernel, n_heads=n_heads),
        grid=(m // tm,),
        in_specs=[pl.BlockSpec((tm, w), lambda i: (i, 0)),
                  pl.BlockSpec((1, w), lambda i: (0, 0)),
                  pl.BlockSpec((tm, GQA_HD), lambda i: (i, 0)),
                  pl.BlockSpec((tm, GQA_HD), lambda i: (i, 0)),
                  pl.BlockSpec((tm, GQA_HD), lambda i: (i, 0))],
        out_specs=pl.BlockSpec((tm, w), lambda i: (i, 0)),
        out_shape=jax.ShapeDtypeStruct((m, w), F32),
        compiler_params=_params(("parallel",)),
        name="qk_norm_rope",
    )(qkv, gains, cos, sa, sb)


def rope_tables():
    t = jnp.arange(DEC_SEQ)
    row = (t // GRID_W).astype(F32)
    col = (t % GRID_W).astype(F32)
    half = GQA_HD // 2
    freqs = ROPE_THETA ** (-jnp.arange(0, half, 2, dtype=F32) / half)
    ar = row[:, None] * freqs
    ac = col[:, None] * freqs
    z = jnp.zeros_like(ar)
    cos = jnp.concatenate([jnp.cos(ar), jnp.cos(ar), jnp.cos(ac), jnp.cos(ac)], axis=-1)
    sa = jnp.concatenate([-jnp.sin(ar), z, -jnp.sin(ac), z], axis=-1)
    sb = jnp.concatenate([z, jnp.sin(ar), z, jnp.sin(ac)], axis=-1)
    ones = jnp.ones((N_PROMPT, GQA_HD), F32)
    zeros = jnp.zeros((N_PROMPT, GQA_HD), F32)
    tile = lambda a: jnp.tile(a, (DEC_BATCH, 1))
    return (jnp.concatenate([ones, tile(cos)], 0), jnp.concatenate([zeros, tile(sa)], 0),
            jnp.concatenate([zeros, tile(sb)], 0))


def _mlstm_kernel(*refs, seq, zero_state):
    if zero_state:
        (q_ref, k_ref, v_ref, og_ref, gt_ref, bg_ref, gh_ref,
         a_ref, cf_ref, nf_ref, mf_ref, hf_ref, hb_ref, cs_ref, ns_ref, m_ref, y_ref, yt_ref) = refs
    else:
        (into_ref, q_ref, k_ref, v_ref, og_ref, gt_ref, bg_ref, gh_ref, c0_ref, n0_ref, m0_ref,
         a_ref, hf_ref, hb_ref, cs_ref, ns_ref, m_ref, y_ref, yt_ref) = refs
        del into_ref
    L = MLSTM_CHUNK
    nc = seq // L
    head = pl.program_id(1)
    scale = MLSTM_DQK ** -0.5

    if zero_state:
        cs_ref[...] = jnp.zeros_like(cs_ref)
        ns_ref[...] = jnp.zeros_like(ns_ref)
        m_ref[...] = jnp.full_like(m_ref, NEG)
    else:
        cs_ref[...] = c0_ref[0, :, 0]
        ns_ref[...] = n0_ref[0, :, 0]
        m_ref[...] = m0_ref[0, :, 0]

    rr = _iota((LANE, LANE), 0)
    cc = _iota((LANE, LANE), 1)
    sel = jnp.where((cc < 4) & (rr == MLSTM_HEADS * cc + head), 1.0, 0.0).astype(F32)
    lane = _iota((L, LANE), 1)
    jj = _iota((L, L), 0)
    ss = _iota((L, L), 1)
    lower = (ss <= jj)
    upper = (ss >= jj)
    lowerf = lower.astype(F32)

    def gate_terms(c, carry):
        c0 = pl.multiple_of(c * L, L)
        g = gt_ref[pl.ds(c0, L), :] + bg_ref[...]
        x = jnp.dot(g, sel, preferred_element_type=F32, precision=HIGHEST)
        logsig = jnp.minimum(x, 0.0) - jnp.log(1.0 + jnp.exp(-jnp.abs(x)))
        x = jnp.where((lane == 1) | (lane == 3), logsig, x)
        pre = jnp.dot(lowerf, x, preferred_element_type=F32, precision=HIGHEST)
        suf = pre[L - 1:L, :] - pre + x
        y = jnp.where(lane == 1, pre, jnp.where(lane == 3, suf, x))
        y_ref[pl.ds(c0, L), :] = y
        yt_ref[pl.ds(pl.multiple_of(c * 8, 8), 8), :] = y.T[:8, :]
        return carry

    lax.fori_loop(0, nc, gate_terms, 0, unroll=4)

    def direction(d, c, c0):
        y = y_ref[pl.ds(c0, L), :]
        yt = yt_ref[pl.ds(pl.multiple_of(c * 8, 8), 8), :]
        li_col, b_col = y[:, 2 * d:2 * d + 1], y[:, 2 * d + 1:2 * d + 2]
        li_row, b_row = yt[2 * d:2 * d + 1, :], yt[2 * d + 1:2 * d + 2, :]
        bl = b_col[L - 1:L, :] if d == 0 else b_col[0:1, :]
        m = m_ref[d][:, :1]
        dlog = jnp.where(lower if d == 0 else upper, b_col - b_row + li_row, -jnp.inf)
        inter = b_col + m
        mj = jnp.maximum(inter, jnp.max(dlog, axis=-1, keepdims=True))
        dw = jnp.exp(dlog - mj)
        iw = jnp.exp(inter - mj)
        qc = q_ref[pl.ds(c0, L), :] * scale
        kc = k_ref[pl.ds(c0, L), :]
        qb = qc.astype(BF16)
        vb = v_ref[pl.ds(c0, L), :].astype(BF16)
        sc = lax.dot_general(qb, kc.astype(BF16), NT, preferred_element_type=F32) * dw
        cs = cs_ref[d]
        ns = ns_ref[d]
        num = (jnp.dot(sc.astype(BF16), vb, preferred_element_type=F32)
               + iw * jnp.dot(qb, cs.astype(BF16), preferred_element_type=F32))
        den = jnp.sum(sc, axis=-1, keepdims=True) + iw * jnp.sum(qc * ns, axis=-1, keepdims=True)
        hc = num / jnp.maximum(jnp.abs(den), jnp.exp(-mj))
        elog = bl - b_col + li_col
        carry_log = bl + m
        m_new = jnp.maximum(carry_log, jnp.max(elog, axis=0, keepdims=True))
        ew = jnp.exp(elog - m_new)
        cw = jnp.exp(carry_log - m_new)
        kw = ew * kc
        cs_ref[d] = cw * cs + lax.dot_general(kw.astype(BF16), vb, TN, preferred_element_type=F32)
        ns_ref[d] = cw * ns + jnp.sum(kw, axis=0, keepdims=True)
        m_ref[d] = jnp.broadcast_to(m_new, (1, LANE))
        return hc

    def body(c, carry):
        cf = pl.multiple_of(c * L, L)
        cb = pl.multiple_of((nc - 1 - c) * L, L)
        hf_ref[pl.ds(cf, L), :] = direction(0, c, cf)
        hb_ref[pl.ds(cb, L), :] = direction(1, nc - 1 - c, cb)
        return carry

    lax.fori_loop(0, nc, body, 0)
    hsum = _rms(hf_ref[...] + hb_ref[...], gh_ref[...])
    a_ref[...] = (hsum * jax.nn.sigmoid(og_ref[...])).astype(a_ref.dtype)
    if zero_state:
        cf_ref[0, :, 0] = cs_ref[...]
        nf_ref[0, :, 0] = ns_ref[...]
        mf_ref[0, :, 0] = m_ref[...]


def mlstm_scan(proj, gates, b_gate, g_head, *, n_batch, seq, row0, state=None, into=None):
    nh, dqk, dv = MLSTM_HEADS, MLSTM_DQK, MLSTM_DV
    rb0 = row0 // seq
    zero_state = state is None
    in_specs = [pl.BlockSpec((seq, dqk), lambda b, h: (rb0 + b, h)),
                pl.BlockSpec((seq, dqk), lambda b, h: (rb0 + b, nh + h)),
                pl.BlockSpec((seq, dv), lambda b, h: (rb0 + b, nh + h)),
                pl.BlockSpec((seq, dv), lambda b, h: (rb0 + b, 2 * nh + h)),
                pl.BlockSpec((seq, LANE), lambda b, h: (rb0 + b, 0)),
                pl.BlockSpec((1, LANE), lambda b, h: (0, 0)),
                pl.BlockSpec((1, dv), lambda b, h: (0, h))]
    bg = jnp.zeros((1, LANE), F32).at[0, :4 * nh].set(b_gate)
    args = [proj, proj, proj, proj, gates, bg, g_head.reshape(1, nh * dv)]
    out_specs = [pl.BlockSpec((seq, dv), lambda b, h: (rb0 + b, h))]
    out_shape = [jax.ShapeDtypeStruct((N_TOK, nh * dv), BF16)]
    c_spec = pl.BlockSpec((1, 2, 1, dqk, dv), lambda b, h: (b, 0, h, 0, 0))
    n_spec = pl.BlockSpec((1, 2, 1, 1, dqk), lambda b, h: (b, 0, h, 0, 0))
    aliases = {}
    if zero_state:
        out_specs += [c_spec, n_spec, n_spec]
        out_shape += [jax.ShapeDtypeStruct((n_batch, 2, nh, dqk, dv), F32),
                      jax.ShapeDtypeStruct((n_batch, 2, nh, 1, dqk), F32),
                      jax.ShapeDtypeStruct((n_batch, 2, nh, 1, LANE), F32)]
    else:
        c0, n0, m0 = state
        in_specs = [pl.BlockSpec(memory_space=pl.ANY)] + in_specs + [c_spec, n_spec, n_spec]
        args = [into] + args + [c0, n0.reshape(n_batch, 2, nh, 1, dqk),
                                jnp.broadcast_to(m0[..., None, None], (n_batch, 2, nh, 1, LANE))]
        aliases = {0: 0}
    return pl.pallas_call(
        functools.partial(_mlstm_kernel, seq=seq, zero_state=zero_state),
        grid=(n_batch, nh),
        in_specs=in_specs,
        out_specs=out_specs,
        out_shape=out_shape,
        input_output_aliases=aliases,
        scratch_shapes=[pltpu.VMEM((seq, dv), F32), pltpu.VMEM((seq, dv), F32),
                        pltpu.VMEM((2, dqk, dv), F32), pltpu.VMEM((2, 1, dqk), F32), pltpu.VMEM((2, 1, LANE), F32),
                        pltpu.VMEM((seq, LANE), F32), pltpu.VMEM((seq // MLSTM_CHUNK * 8, MLSTM_CHUNK), F32)],
        compiler_params=_params(("parallel", "parallel")),
        name="mlstm_scan_ctx" if zero_state else "mlstm_scan_lat",
    )(*args)


def _conv_silu_kernel(*refs, seq, aliased):
    if aliased:
        refs = refs[1:]
    x_ref, w_ref, b_ref, o_ref = refs
    x = x_ref[...]
    t = _iota((seq, 1), 0)
    half = SSD_CONV // 2
    acc = b_ref[...] + w_ref[half:half + 1, :] * x
    for k in range(SSD_CONV):
        off = k - half
        if off == 0:
            continue
        shifted = pltpu.roll(x, (-off) % seq, axis=0)
        valid = (t + off >= 0) & (t + off < seq)
        acc = acc + w_ref[k:k + 1, :] * jnp.where(valid, shifted, 0.0)
    o_ref[...] = acc * jax.nn.sigmoid(acc)


def conv_silu(proj, conv_w, conv_b, *, n_batch, seq, row0, col0, into=None, tc=512):
    ch = conv_w.shape[1]
    rb0 = row0 // seq
    aliased = into is not None
    in_specs = [pl.BlockSpec((seq, tc), lambda b, j: (rb0 + b, col0 // tc + j)),
                pl.BlockSpec((SSD_CONV, tc), lambda b, j: (0, j)),
                pl.BlockSpec((1, tc), lambda b, j: (0, j))]
    args = [proj, conv_w, conv_b.reshape(1, ch)]
    if aliased:
        in_specs = [pl.BlockSpec(memory_space=pl.ANY)] + in_specs
        args = [into] + args
    return pl.pallas_call(
        functools.partial(_conv_silu_kernel, seq=seq, aliased=aliased),
        grid=(n_batch, ch // tc),
        in_specs=in_specs,
        out_specs=pl.BlockSpec((seq, tc), lambda b, j: (rb0 + b, j)),
        out_shape=jax.ShapeDtypeStruct((N_TOK, ch), F32),
        input_output_aliases={0: 0} if aliased else {},
        compiler_params=_params(("parallel", "parallel")),
        name="ssd_conv",
    )(*args)


def _ssd_kernel(*refs, seq, zero_state):
    if zero_state:
        (xs_ref, bm_ref, cm_ref, dt_ref, z_ref, par_ref, dsk_ref,
         y_ref, ssq_ref, sf_ref, yb_ref, st_ref) = refs
    else:
        (into_y, into_q, xs_ref, bm_ref, cm_ref, dt_ref, z_ref, par_ref, dsk_ref, s0_ref,
         y_ref, ssq_ref, yb_ref, st_ref) = refs
        del into_y, into_q
    L = SSD_CHUNK
    nc = seq // L
    R, P = SSD_R, SSD_HEADDIM
    grp = pl.program_id(1)

    if zero_state:
        st_ref[...] = jnp.zeros_like(st_ref)
    else:
        st_ref[...] = s0_ref[0]

    rr = _iota((LANE, LANE), 0)
    cc = _iota((LANE, LANE), 1)
    sel = jnp.where((cc < 2 * R) & (rr == (cc // R) * SSD_HEADS + grp * R + cc % R), 1.0, 0.0).astype(F32)
    lane = _iota((L, LANE), 1)
    ll = _iota((L, L), 0)
    ss = _iota((L, L), 1)
    lower = (ss <= ll)
    upper = (ss >= ll)
    lowerf = lower.astype(F32)
    dt_bias = par_ref[0, 0:1, :]
    a_neg = -jnp.exp(par_ref[0, 1:2, :])

    def direction(d, c0, out_ref):
        dtc = _softplus(jnp.dot(dt_ref[pl.ds(c0, L), :], sel, preferred_element_type=F32, precision=HIGHEST)
                        + dt_bias)
        dac = dtc * a_neg
        pre = jnp.dot(lowerf, dac, preferred_element_type=F32, precision=HIGHEST)
        tot = pre[L - 1:L, :]
        acum = pre if d == 0 else tot - pre + dac
        acum_t = acum.T
        bmc = bm_ref[pl.ds(c0, L), :].astype(BF16)
        cmc = cm_ref[pl.ds(c0, L), :].astype(BF16)
        cb = lax.dot_general(cmc, bmc, NT, preferred_element_type=F32)
        mask = lower if d == 0 else upper
        for r in range(R):
            u = d * R + r
            col = acum[:, u:u + 1]
            row = acum_t[u:u + 1, :]
            lmat = jnp.exp(jnp.where(mask, col - row, -jnp.inf))
            last = tot[:, u:u + 1]
            xr = xs_ref[pl.ds(c0, L), r * P:(r + 1) * P] * dtc[:, u:u + 1]
            y_diag = jnp.dot((cb * lmat).astype(BF16), xr.astype(BF16), preferred_element_type=F32)
            start = st_ref[d, r]
            y_off = lax.dot_general(cmc, start.astype(BF16), NT, preferred_element_type=F32) * jnp.exp(col)
            decay = jnp.exp(last - col)
            new = lax.dot_general((xr * decay).astype(BF16), bmc, TN, preferred_element_type=F32)
            st_ref[d, r] = jnp.exp(last) * start + new
            out_ref[pl.ds(c0, L), r * P:(r + 1) * P] = y_diag + y_off

    def body(c, carry):
        cf = pl.multiple_of(c * L, L)
        cbk = pl.multiple_of((nc - 1 - c) * L, L)
        direction(0, cf, y_ref)
        direction(1, cbk, yb_ref)
        return carry

    lax.fori_loop(0, nc, body, 0)
    z = z_ref[...]
    y = (y_ref[...] + yb_ref[...] + dsk_ref[...] * xs_ref[...]) * (z * jax.nn.sigmoid(z))
    y_ref[...] = y
    part = jnp.broadcast_to(jnp.sum(y * y, axis=-1, keepdims=True), ssq_ref.shape)

    @pl.when(grp == 0)
    def _():
        ssq_ref[...] = part

    @pl.when(grp > 0)
    def _():
        ssq_ref[...] += part

    if zero_state:
        sf_ref[0] = st_ref[...]


def ssd_scan(proj, xbc, dt_bias, a_log, d_skip, *, n_batch, seq, row0, state=None, into=None):
    di, gw, ng, r = SSD_D_INNER, SSD_GW, SSD_GROUPS, SSD_R
    rb0 = row0 // seq
    zero_state = state is None
    dt_blk = (2 * di + 2 * ng * SSD_STATE) // LANE
    pack = lambda p: p.reshape(2, ng, r).transpose(1, 0, 2).reshape(ng, 2 * r)
    par = jnp.zeros((ng, 8, LANE), F32).at[:, 0, :2 * r].set(pack(dt_bias)).at[:, 1, :2 * r].set(pack(a_log))
    dsk = jnp.repeat(d_skip, SSD_HEADDIM).reshape(1, di)
    in_specs = [pl.BlockSpec((seq, gw), lambda b, g: (rb0 + b, g)),
                pl.BlockSpec((seq, SSD_STATE), lambda b, g: (rb0 + b, di // SSD_STATE + g)),
                pl.BlockSpec((seq, SSD_STATE), lambda b, g: (rb0 + b, di // SSD_STATE + ng + g)),
                pl.BlockSpec((seq, LANE), lambda b, g: (rb0 + b, dt_blk)),
                pl.BlockSpec((seq, gw), lambda b, g: (rb0 + b, g)),
                pl.BlockSpec((1, 8, LANE), lambda b, g: (g, 0, 0)),
                pl.BlockSpec((1, gw), lambda b, g: (0, g))]
    args = [xbc, xbc, xbc, proj, proj, par, dsk]
    out_specs = [pl.BlockSpec((seq, gw), lambda b, g: (rb0 + b, g)),
                 pl.BlockSpec((seq, LANE), lambda b, g: (rb0 + b, 0))]
    out_shape = [jax.ShapeDtypeStruct((N_TOK, di), F32), jax.ShapeDtypeStruct((N_TOK, LANE), F32)]
    s_spec = pl.BlockSpec((1, 2, r, SSD_HEADDIM, SSD_STATE), lambda b, g: (b, 0, g, 0, 0))
    aliases = {}
    if zero_state:
        out_specs.append(s_spec)
        out_shape.append(jax.ShapeDtypeStruct((n_batch, 2, SSD_HEADS, SSD_HEADDIM, SSD_STATE), F32))
    else:
        in_specs = [pl.BlockSpec(memory_space=pl.ANY)] * 2 + in_specs + [s_spec]
        args = list(into) + args + [state]
        aliases = {0: 0, 1: 1}
    return pl.pallas_call(
        functools.partial(_ssd_kernel, seq=seq, zero_state=zero_state),
        grid=(n_batch, ng),
        in_specs=in_specs,
        out_specs=out_specs,
        out_shape=out_shape,
        input_output_aliases=aliases,
        scratch_shapes=[pltpu.VMEM((seq, gw), F32), pltpu.VMEM((2, r, SSD_HEADDIM, SSD_STATE), F32)],
        compiler_params=_params(("parallel", "arbitrary")),
        name="ssd_scan_ctx" if zero_state else "ssd_scan_lat",
    )(*args)


def kernel(x_prompt, x_sample, state_mlstm_c, state_mlstm_n, state_mlstm_m, cache_nat_k, cache_nat_v, cache_gqa_k, cache_gqa_v, state_ssd, c, c_ctx, w_mod, b_mod, norm_g, mlstm_w_in, mlstm_b_gate, mlstm_g_head, mlstm_w_out, nat_w_qkv, nat_rpb, nat_w_out, gqa_w_qkv, gqa_q_g, gqa_k_g, gqa_w_out, ssd_w_in, ssd_conv_w, ssd_conv_b, ssd_dt_bias, ssd_a_log, ssd_d, ssd_g_norm, ssd_w_out, ffn_w_in, ffn_w_out, moe_w_router, moe_b_router, moe_w_in, moe_w_out):
    d = D_MODEL
    h = jnp.concatenate([x_prompt.reshape(N_PROMPT, d), x_sample.reshape(N_LATENT, d)], axis=0)
    cvecs = jnp.zeros((8, d), F32).at[0].set(c_ctx).at[1:1 + DEC_BATCH].set(c)
    mod = modulation_all(cvecs, w_mod, b_mod)[:, :N_GROUPS].reshape(DEPTH, N_GROUPS, 6, 1, d)
    outs = {}
    for layer in range(DEPTH):
        kind, j = layer % N_MIXERS, layer // N_MIXERS
        sh1, sc1, g1, sh2, sc2, g2 = [mod[layer, :, i] for i in range(6)]
        ng = norm_g[layer]
        if kind == 0:
            n_main = 2 * MLSTM_HEADS * MLSTM_DQK + 2 * MLSTM_HEADS * MLSTM_DV
            proj = norm_linear(h, ng[0], sc1, sh1, mlstm_w_in, j, n_out=n_main, name="mlstm_in")
            w_gates = jnp.zeros((1, d, LANE), F32).at[0, :, :4 * MLSTM_HEADS].set(mlstm_w_in[j, :, n_main:])
            gates = norm_linear(h, ng[0], sc1, sh1, w_gates, 0, tn=LANE, name="mlstm_gates")
            a, st_c, st_n, st_m = mlstm_scan(proj, gates, mlstm_b_gate[j], mlstm_g_head[j],
                                             n_batch=BATCH, seq=SEQ, row0=0)
            a = mlstm_scan(proj, gates, mlstm_b_gate[j], mlstm_g_head[j], n_batch=DEC_BATCH, seq=DEC_SEQ,
                           row0=N_PROMPT, state=(state_mlstm_c[:, j], state_mlstm_n[:, j], state_mlstm_m[:, j]),
                           into=a)[0]
            outs['mc'] = st_c[:, None]
            outs['mn'] = st_n[:, None, :, :, 0]
            outs['mm'] = st_m[:, None, :, :, 0, 0]
            h = linear_out(a, mlstm_w_out, j, h, ng[1], g1, name="mlstm_out")
        elif kind == 1:
            nh = NAT_HEADS
            hw = nh * NAT_HD
            qkv = norm_linear(h, ng[0], sc1, sh1, nat_w_qkv, j, name="nat_qkv")
            a = attention(qkv, qkv, qkv, n_batch=BATCH, n_heads=nh, nq=SEQ, nk=SEQ, tq=SEQ, row0=0,
                          q_col=lambda hh: hh, k_col=lambda hh: nh + hh, v_col=lambda hh: 2 * nh + hh,
                          name="nat_ctx_attn")
            a = nat_latent(qkv, cache_nat_k[:, j].reshape(DEC_BATCH, PAST_LEN, hw),
                           cache_nat_v[:, j].reshape(DEC_BATCH, PAST_LEN, hw), nat_bias_slabs(nat_rpb[j]), a)
            outs['nk'] = qkv[:N_PROMPT, hw:2 * hw].reshape(BATCH, 1, SEQ, nh, NAT_HD)
            outs['nv'] = qkv[:N_PROMPT, 2 * hw:].reshape(BATCH, 1, SEQ, nh, NAT_HD)
            h = linear_out(a, nat_w_out, j, h, ng[1], g1, name="nat_out")
        elif kind == 2:
            nqk = GQA_HEADS + GQA_KV_HEADS
            grp = GQA_HEADS // GQA_KV_HEADS
            kw = GQA_KV_HEADS * GQA_HD
            qkv = norm_linear(h, ng[0], sc1, sh1, gqa_w_qkv, j, name="gqa_qkv")
            gains = jnp.concatenate([jnp.tile(gqa_q_g[j], GQA_HEADS), jnp.tile(gqa_k_g[j], GQA_KV_HEADS)])
            cos, sa, sb = rope_tables()
            qk = qk_norm_rope(qkv, gains.reshape(1, -1), cos, sa, sb, n_heads=nqk)
            cols = dict(q_col=lambda hh: hh, k_col=lambda hh: GQA_HEADS + hh // grp,
                        v_col=lambda hh: nqk + hh // grp)
            a = attention(qk, qk, qkv, n_batch=BATCH, n_heads=GQA_HEADS, nq=SEQ, nk=SEQ, tq=SEQ, row0=0,
                          name="gqa_ctx_attn", **cols)
            a = attention(qk, qk, qkv, n_batch=DEC_BATCH, n_heads=GQA_HEADS, nq=DEC_SEQ, nk=DEC_SEQ, tq=256,
                          row0=N_PROMPT, k_cache=cache_gqa_k[:, j].reshape(DEC_BATCH, PAST_LEN, kw),
                          v_cache=cache_gqa_v[:, j].reshape(DEC_BATCH, PAST_LEN, kw),
                          kv_of_head=lambda hh: hh // grp, into=a, name="gqa_lat_attn", **cols)
            outs['gk'] = qk[:N_PROMPT, GQA_HEADS * GQA_HD:].reshape(BATCH, 1, SEQ, GQA_KV_HEADS, GQA_HD)
            outs['gv'] = qkv[:N_PROMPT, GQA_HEADS * GQA_HD + kw:].reshape(BATCH, 1, SEQ, GQA_KV_HEADS, GQA_HD)
            h = linear_out(a, gqa_w_out, j, h, ng[1], g1, name="gqa_out")
        else:
            proj = norm_linear(h, ng[0], sc1, sh1, ssd_w_in, j, tn=384, name="ssd_in")
            conv = dict(col0=SSD_D_INNER)
            xbc = conv_silu(proj, ssd_conv_w[j], ssd_conv_b[j], n_batch=BATCH, seq=SEQ, row0=0, **conv)
            xbc = conv_silu(proj, ssd_conv_w[j], ssd_conv_b[j], n_batch=DEC_BATCH, seq=DEC_SEQ, row0=N_PROMPT,
                            into=xbc, **conv)
            y, ssq, st_s = ssd_scan(proj, xbc, ssd_dt_bias[j], ssd_a_log[j], ssd_d[j],
                                    n_batch=BATCH, seq=SEQ, row0=0)
            y, ssq = ssd_scan(proj, xbc, ssd_dt_bias[j], ssd_a_log[j], ssd_d[j], n_batch=DEC_BATCH, seq=DEC_SEQ,
                              row0=N_PROMPT, state=state_ssd[:, j], into=(y, ssq))
            outs['ss'] = st_s[:, None]
            h = linear_out(y, ssd_w_out, j, h, ng[1], g1, ssq=ssq, gn=ssd_g_norm[j], name="ssd_out")
        e = layer // 2
        if layer % 2 == 0:
            act = norm_swiglu_in(h, ng[2], sc2, sh2, ffn_w_in, e)
            h = linear_out(act, ffn_w_out, e, h, ng[3], g2, name="ffn_out")
        else:
            h = moe_layer(h, ng[2], sc2, sh2, moe_w_router[e], moe_b_router[e], moe_w_in, moe_w_out, e,
                          ng[3], g2)
    y_prompt = h[:N_PROMPT].reshape(BATCH, SEQ, d)
    y_sample = h[N_PROMPT:].reshape(DEC_BATCH, DEC_SEQ, d)
    return (y_prompt, y_sample, outs['mc'], outs['mn'], outs['mm'], outs['nk'], outs['nv'],
            outs['gk'], outs['gv'], outs['ss'])
```

```python
import functools

import jax
import jax.numpy as jnp
from jax import lax
from jax.experimental import pallas as pl
from jax.experimental.pallas import tpu as pltpu

D_MODEL = 2048
BATCH = 16
SEQ = 256
DEPTH = 4
DEC_BATCH = 2
DEC_SEQ = 2048
PAST_LEN = 512
GRID_W = 64
N_MIXERS = 4

MLSTM_HEADS = 8
MLSTM_DV = D_MODEL // MLSTM_HEADS
MLSTM_DQK = MLSTM_DV // 2
MLSTM_CHUNK = 64

NAT_HEADS = 16
NAT_HD = D_MODEL // NAT_HEADS
NAT_KH = 8
NAT_KW = 16

GQA_HEADS = 16
GQA_KV_HEADS = 4
GQA_HD = D_MODEL // GQA_HEADS
ROPE_THETA = 10000.0

SSD_D_INNER = 2 * D_MODEL
SSD_HEADDIM = 64
SSD_HEADS = SSD_D_INNER // SSD_HEADDIM
SSD_GROUPS = 8
SSD_STATE = 128
SSD_CONV = 5
SSD_CHUNK = 128
SSD_R = SSD_HEADS // SSD_GROUPS
SSD_GW = SSD_R * SSD_HEADDIM

D_FF = 7 * D_MODEL // 2
N_EXPERTS = 8
TOP_K = 2

EPS = 1e-6
NEG = -1e30

N_PROMPT = BATCH * SEQ
N_LATENT = DEC_BATCH * DEC_SEQ
N_TOK = N_PROMPT + N_LATENT
N_GROUPS = 1 + DEC_BATCH
LANE = 128
VMEM_LIMIT = 56 * 1024 * 1024

F32 = jnp.float32
BF16 = jnp.bfloat16
HIGHEST = lax.Precision.HIGHEST
NT = (((1,), (1,)), ((), ()))
TN = (((0,), (0,)), ((), ()))


def _params(sem, vmem=VMEM_LIMIT):
    return pltpu.CompilerParams(dimension_semantics=sem, vmem_limit_bytes=vmem)


def _group_of_tile(i, tm):
    return jnp.maximum(i * tm // DEC_SEQ - (N_PROMPT // DEC_SEQ - 1), 0)


def _rms(x, g):
    return x * lax.rsqrt(jnp.mean(x * x, axis=-1, keepdims=True) + EPS) * g


def _softplus(x):
    return jnp.maximum(x, 0.0) + jnp.log(1.0 + jnp.exp(-jnp.abs(x)))


def _iota(shape, dim):
    return lax.broadcasted_iota(jnp.int32, shape, dim)


def _mod_kernel(c_ref, w_ref, b_ref, o_ref):
    c = c_ref[...]
    s = (c * jax.nn.sigmoid(c)).astype(BF16)
    o_ref[0] = jnp.dot(s, w_ref[0].astype(BF16), preferred_element_type=F32) + b_ref[0]


def modulation_all(cvecs, w_mod, b_mod):
    tn = 1024
    n = w_mod.shape[-1]
    return pl.pallas_call(
        _mod_kernel,
        grid=(DEPTH, n // tn),
        in_specs=[pl.BlockSpec((8, D_MODEL), lambda l, j: (0, 0)),
                  pl.BlockSpec((1, D_MODEL, tn), lambda l, j: (l, 0, j)),
                  pl.BlockSpec((1, 1, tn), lambda l, j: (l, 0, j))],
        out_specs=pl.BlockSpec((1, 8, tn), lambda l, j: (l, 0, j)),
        out_shape=jax.ShapeDtypeStruct((DEPTH, 8, n), F32),
        compiler_params=_params(("parallel", "parallel")),
        name="modulation",
    )(cvecs, w_mod, b_mod.reshape(DEPTH, 1, n))


def _norm_linear_kernel(x_ref, g_ref, sc_ref, sh_ref, w_ref, o_ref, u_ref):
    @pl.when(pl.program_id(1) == 0)
    def _():
        u = _rms(x_ref[...], g_ref[...]) * (1.0 + sc_ref[0]) + sh_ref[0]
        u_ref[...] = u.astype(BF16)

    o_ref[...] = jnp.dot(u_ref[...], w_ref[0].astype(BF16), preferred_element_type=F32).astype(o_ref.dtype)


def norm_linear(h, g, scale, shift, w, wl, *, n_out=None, tm=1024, tn=512, out_dtype=F32, name="norm_linear"):
    m, d = h.shape
    n_out = w.shape[2] if n_out is None else n_out
    return pl.pallas_call(
        _norm_linear_kernel,
        grid=(m // tm, pl.cdiv(n_out, tn)),
        in_specs=[pl.BlockSpec((tm, d), lambda i, j: (i, 0)),
                  pl.BlockSpec((1, d), lambda i, j: (0, 0)),
                  pl.BlockSpec((1, 1, d), lambda i, j: (_group_of_tile(i, tm), 0, 0)),
                  pl.BlockSpec((1, 1, d), lambda i, j: (_group_of_tile(i, tm), 0, 0)),
                  pl.BlockSpec((1, d, tn), lambda i, j: (wl, 0, j))],
        out_specs=pl.BlockSpec((tm, tn), lambda i, j: (i, j)),
        out_shape=jax.ShapeDtypeStruct((m, n_out), out_dtype),
        scratch_shapes=[pltpu.VMEM((tm, d), BF16)],
        compiler_params=_params(("parallel", "arbitrary")),
        name=name,
    )(h, g.reshape(1, d), scale, shift, w)


def _norm_swiglu_kernel(x_ref, g_ref, sc_ref, sh_ref, wg_ref, wu_ref, o_ref, u_ref):
    @pl.when(pl.program_id(1) == 0)
    def _():
        u = _rms(x_ref[...], g_ref[...]) * (1.0 + sc_ref[0]) + sh_ref[0]
        u_ref[...] = u.astype(BF16)

    u = u_ref[...]
    a = jnp.dot(u, wg_ref[0].astype(BF16), preferred_element_type=F32)
    b = jnp.dot(u, wu_ref[0].astype(BF16), preferred_element_type=F32)
    o_ref[...] = (a * jax.nn.sigmoid(a) * b).astype(o_ref.dtype)


def norm_swiglu_in(h, g, scale, shift, w_in, wl, *, tm=1024, tn=256):
    m, d = h.shape
    f = w_in.shape[2] // 2
    nj = f // tn
    return pl.pallas_call(
        _norm_swiglu_kernel,
        grid=(m // tm, nj),
        in_specs=[pl.BlockSpec((tm, d), lambda i, j: (i, 0)),
                  pl.BlockSpec((1, d), lambda i, j: (0, 0)),
                  pl.BlockSpec((1, 1, d), lambda i, j: (_group_of_tile(i, tm), 0, 0)),
                  pl.BlockSpec((1, 1, d), lambda i, j: (_group_of_tile(i, tm), 0, 0)),
                  pl.BlockSpec((1, d, tn), lambda i, j: (wl, 0, j)),
                  pl.BlockSpec((1, d, tn), lambda i, j: (wl, 0, j + nj))],
        out_specs=pl.BlockSpec((tm, tn), lambda i, j: (i, j)),
        out_shape=jax.ShapeDtypeStruct((m, f), BF16),
        scratch_shapes=[pltpu.VMEM((tm, d), BF16)],
        compiler_params=_params(("parallel", "arbitrary")),
        name="norm_swiglu_in",
    )(h, g.reshape(1, d), scale, shift, w_in, w_in)


def _linear_out_kernel(*refs, prescale, kdim):
    if prescale:
        a_ref, w_ref, h_ref, g_ref, gate_ref, ssq_ref, gn_ref, o_ref, acc_ref = refs
    else:
        a_ref, w_ref, h_ref, g_ref, gate_ref, o_ref, acc_ref = refs
    k = pl.program_id(1)

    @pl.when(k == 0)
    def _():
        acc_ref[...] = jnp.zeros_like(acc_ref)

    a = a_ref[...]
    if prescale:
        a = a * lax.rsqrt(ssq_ref[:, :1] * (1.0 / kdim) + EPS) * gn_ref[...]
    acc_ref[...] += jnp.dot(a.astype(BF16), w_ref[0].astype(BF16), preferred_element_type=F32)

    @pl.when(k == pl.num_programs(1) - 1)
    def _():
        o_ref[...] = h_ref[...] + gate_ref[0] * _rms(acc_ref[...], g_ref[...])


def linear_out(a, w, wl, h, g, gate, *, ssq=None, gn=None, tm=512, tk=512, name="linear_out"):
    m, kdim = a.shape
    d = w.shape[2]
    prescale = ssq is not None
    in_specs = [pl.BlockSpec((tm, tk), lambda i, k: (i, k)),
                pl.BlockSpec((1, tk, d), lambda i, k: (wl, k, 0)),
                pl.BlockSpec((tm, d), lambda i, k: (i, 0)),
                pl.BlockSpec((1, d), lambda i, k: (0, 0)),
                pl.BlockSpec((1, 1, d), lambda i, k: (_group_of_tile(i, tm), 0, 0))]
    args = [a, w, h, g.reshape(1, d), gate]
    if prescale:
        in_specs += [pl.BlockSpec((tm, LANE), lambda i, k: (i, 0)),
                     pl.BlockSpec((1, tk), lambda i, k: (0, k))]
        args += [ssq, gn.reshape(1, kdim)]
    return pl.pallas_call(
        functools.partial(_linear_out_kernel, prescale=prescale, kdim=kdim),
        grid=(m // tm, kdim // tk),
        in_specs=in_specs,
        out_specs=pl.BlockSpec((tm, d), lambda i, k: (i, 0)),
        out_shape=jax.ShapeDtypeStruct((m, d), F32),
        scratch_shapes=[pltpu.VMEM((tm, d), F32)],
        compiler_params=_params(("parallel", "arbitrary")),
        name=name,
    )(*args)


def _norm_router_kernel(x_ref, g_ref, sc_ref, sh_ref, wr_ref, br_ref, u_ref, rt_ref):
    u = _rms(x_ref[...], g_ref[...]) * (1.0 + sc_ref[0]) + sh_ref[0]
    u_ref[...] = u
    lg = jnp.dot(u, wr_ref[...], preferred_element_type=F32, precision=HIGHEST) + br_ref[...]
    lane = _iota(lg.shape, 1)
    lanef = lane.astype(F32)
    lg = jnp.where(lane < N_EXPERTS, lg, -jnp.inf)
    m1 = jnp.max(lg, axis=-1, keepdims=True)
    i1 = jnp.min(jnp.where(lg == m1, lanef, float(LANE)), axis=-1, keepdims=True)
    lg2 = jnp.where(lanef == i1, -jnp.inf, lg)
    m2 = jnp.max(lg2, axis=-1, keepdims=True)
    i2 = jnp.min(jnp.where(lg2 == m2, lanef, float(LANE)), axis=-1, keepdims=True)
    e = jnp.exp(m2 - m1)
    w1 = 1.0 / (1.0 + e)
    w2 = e / (1.0 + e)
    rt_ref[...] = jnp.where(lane == 0, i1, jnp.where(lane == 1, i2, jnp.where(lane == 2, w1,
                            jnp.where(lane == 3, w2, 0.0))))


def norm_router(h, g, scale, shift, w_router, b_router, *, tm=512):
    m, d = h.shape
    wr = jnp.zeros((d, LANE), F32).at[:, :N_EXPERTS].set(w_router)
    br = jnp.zeros((1, LANE), F32).at[0, :N_EXPERTS].set(b_router)
    return pl.pallas_call(
        _norm_router_kernel,
        grid=(m // tm,),
        in_specs=[pl.BlockSpec((tm, d), lambda i: (i, 0)),
                  pl.BlockSpec((1, d), lambda i: (0, 0)),
                  pl.BlockSpec((1, 1, d), lambda i: (_group_of_tile(i, tm), 0, 0)),
                  pl.BlockSpec((1, 1, d), lambda i: (_group_of_tile(i, tm), 0, 0)),
                  pl.BlockSpec((d, LANE), lambda i: (0, 0)),
                  pl.BlockSpec((1, LANE), lambda i: (0, 0))],
        out_specs=[pl.BlockSpec((tm, d), lambda i: (i, 0)),
                   pl.BlockSpec((tm, LANE), lambda i: (i, 0))],
        out_shape=[jax.ShapeDtypeStruct((m, d), F32), jax.ShapeDtypeStruct((m, LANE), F32)],
        compiler_params=_params(("parallel",)),
        name="norm_router",
    )(h, g.reshape(1, d), scale, shift, wr, br)


MOE_TM = 512


def _row_copy(src_hbm, idx, dst, r, sem):
    return pltpu.make_async_copy(src_hbm.at[pl.ds(idx, 1), :], dst.at[pl.ds(r, 1), :], sem)


def _gather_all(idx_ref, base, src_hbm, dst, sem, n):
    def issue(r, c):
        _row_copy(src_hbm, idx_ref[base + r], dst, r, sem).start()
        return c

    lax.fori_loop(0, n, issue, 0, unroll=8)

    def drain(r, c):
        _row_copy(src_hbm, 0, dst, r, sem).wait()
        return c

    lax.fori_loop(0, n, drain, 0, unroll=8)


def _moe_gather_kernel(tok_ref, nu_ref, u_hbm, o_ref, buf, sem):
    @pl.when(pl.program_id(0) < nu_ref[0])
    def _():
        _gather_all(tok_ref, pl.program_id(0) * MOE_TM, u_hbm, buf, sem, MOE_TM)
        o_ref[...] = buf[...].astype(o_ref.dtype)

    @pl.when(pl.program_id(0) >= nu_ref[0])
    def _():
        o_ref[...] = jnp.zeros_like(o_ref)


def moe_gather(u, row_tok, n_used):
    r = row_tok.shape[0]
    d = u.shape[1]
    return pl.pallas_call(
        _moe_gather_kernel,
        grid_spec=pltpu.PrefetchScalarGridSpec(
            num_scalar_prefetch=2,
            grid=(r // MOE_TM,),
            in_specs=[pl.BlockSpec(memory_space=pl.ANY)],
            out_specs=pl.BlockSpec((MOE_TM, d), lambda b, tok, nu: (b, 0)),
            scratch_shapes=[pltpu.VMEM((MOE_TM, d), F32), pltpu.SemaphoreType.DMA(())],
        ),
        out_shape=jax.ShapeDtypeStruct((r, d), BF16),
        compiler_params=_params(("arbitrary",)),
        name="moe_gather",
    )(row_tok, n_used, u)


def _moe_in_kernel(be_ref, nu_ref, x_ref, wg_ref, wu_ref, o_ref):
    @pl.when(pl.program_id(1) < nu_ref[0])
    def _():
        x = x_ref[...]
        a = jnp.dot(x, wg_ref[0, 0].astype(BF16), preferred_element_type=F32)
        b = jnp.dot(x, wu_ref[0, 0].astype(BF16), preferred_element_type=F32)
        o_ref[...] = (a * jax.nn.sigmoid(a) * b).astype(o_ref.dtype)

    @pl.when(pl.program_id(1) >= nu_ref[0])
    def _():
        o_ref[...] = jnp.zeros_like(o_ref)


def _used_block(b, nu):
    return jnp.minimum(b, nu[0] - 1)


def moe_in(x_rows, blk_expert, n_used, w_in, wl, *, tn=512):
    r, d = x_rows.shape
    f = w_in.shape[3] // 2
    nj = f // tn
    nb = r // MOE_TM
    return pl.pallas_call(
        _moe_in_kernel,
        grid_spec=pltpu.PrefetchScalarGridSpec(
            num_scalar_prefetch=2,
            grid=(nj, nb),
            in_specs=[pl.BlockSpec((MOE_TM, d), lambda j, b, be, nu: (_used_block(b, nu), 0)),
                      pl.BlockSpec((1, 1, d, tn), lambda j, b, be, nu: (wl, be[_used_block(b, nu)], 0, j)),
                      pl.BlockSpec((1, 1, d, tn), lambda j, b, be, nu: (wl, be[_used_block(b, nu)], 0, j + nj))],
            out_specs=pl.BlockSpec((MOE_TM, tn), lambda j, b, be, nu: (b, j)),
        ),
        out_shape=jax.ShapeDtypeStruct((r, f), BF16),
        compiler_params=_params(("arbitrary", "arbitrary")),
        name="moe_in",
    )(blk_expert, n_used, x_rows, w_in, w_in)


def _moe_out_kernel(be_ref, nu_ref, a_ref, w_ref, rw_ref, o_ref):
    @pl.when(pl.program_id(1) < nu_ref[0])
    def _():
        y = jnp.dot(a_ref[...], w_ref[0, 0].astype(BF16), preferred_element_type=F32)
        o_ref[...] = y * rw_ref[...]

    @pl.when(pl.program_id(1) >= nu_ref[0])
    def _():
        o_ref[...] = jnp.zeros_like(o_ref)


def moe_out(act_rows, blk_expert, n_used, w_out, wl, row_w, *, tn=256):
    r, f = act_rows.shape
    d = w_out.shape[3]
    nb = r // MOE_TM
    return pl.pallas_call(
        _moe_out_kernel,
        grid_spec=pltpu.PrefetchScalarGridSpec(
            num_scalar_prefetch=2,
            grid=(d // tn, nb),
            in_specs=[pl.BlockSpec((MOE_TM, f), lambda j, b, be, nu: (_used_block(b, nu), 0)),
                      pl.BlockSpec((1, 1, f, tn), lambda j, b, be, nu: (wl, be[_used_block(b, nu)], 0, j)),
                      pl.BlockSpec((MOE_TM, 1), lambda j, b, be, nu: (_used_block(b, nu), 0))],
            out_specs=pl.BlockSpec((MOE_TM, tn), lambda j, b, be, nu: (b, j)),
        ),
        out_shape=jax.ShapeDtypeStruct((r, d), F32),
        compiler_params=_params(("arbitrary", "arbitrary")),
        name="moe_out",
    )(blk_expert, n_used, act_rows, w_out, row_w.reshape(r, 1))


COMBINE_TM = 256


def _moe_combine_kernel(pos_ref, y_hbm, h_ref, g_ref, gate_ref, o_ref, buf, sem):
    n = TOP_K * COMBINE_TM
    _gather_all(pos_ref, pl.program_id(0) * n, y_hbm, buf, sem, n)
    f = buf[:COMBINE_TM, :] + buf[COMBINE_TM:, :]
    o_ref[...] = h_ref[...] + gate_ref[0] * _rms(f, g_ref[...])


def moe_combine(y_rows, pos_blocked, h, g, gate):
    m, d = h.shape
    tm = COMBINE_TM
    return pl.pallas_call(
        _moe_combine_kernel,
        grid_spec=pltpu.PrefetchScalarGridSpec(
            num_scalar_prefetch=1,
            grid=(m // tm,),
            in_specs=[pl.BlockSpec(memory_space=pl.ANY),
                      pl.BlockSpec((tm, d), lambda i, pos: (i, 0)),
                      pl.BlockSpec((1, d), lambda i, pos: (0, 0)),
                      pl.BlockSpec((1, 1, d), lambda i, pos: (_group_of_tile(i, tm), 0, 0))],
            out_specs=pl.BlockSpec((tm, d), lambda i, pos: (i, 0)),
            scratch_shapes=[pltpu.VMEM((TOP_K * tm, d), F32), pltpu.SemaphoreType.DMA(())],
        ),
        out_shape=jax.ShapeDtypeStruct((m, d), F32),
        compiler_params=_params(("arbitrary",)),
        name="moe_combine",
    )(pos_blocked, y_rows, h, g.reshape(1, d), gate)


def moe_layer(h, g, scale, shift, w_router, b_router, w_in, w_out, wl, g_post, gate):
    t = h.shape[0]
    u, route = norm_router(h, g, scale, shift, w_router, b_router)
    expert = route[:, :TOP_K].astype(jnp.int32).reshape(-1)
    weight = route[:, TOP_K:2 * TOP_K].reshape(-1)
    n_assign = t * TOP_K
    onehot = (expert[:, None] == jnp.arange(N_EXPERTS, dtype=jnp.int32)[None, :]).astype(jnp.int32)
    rank = jnp.sum((jnp.cumsum(onehot, axis=0) - onehot) * onehot, axis=1)
    counts = jnp.sum(onehot, axis=0)
    padded = (counts + MOE_TM - 1) // MOE_TM * MOE_TM
    pad_end = jnp.cumsum(padded)
    pad_start = pad_end - padded
    pos = (pad_start[expert] + rank).astype(jnp.int32)
    n_blocks = n_assign // MOE_TM + N_EXPERTS
    n_rows = n_blocks * MOE_TM
    token = jnp.repeat(jnp.arange(t, dtype=jnp.int32), TOP_K)
    row_tok = jnp.zeros((n_rows,), jnp.int32).at[pos].set(token)
    row_w = jnp.zeros((n_rows,), F32).at[pos].set(weight)
    blk_expert = jnp.minimum(
        jnp.searchsorted(pad_end, jnp.arange(n_blocks, dtype=jnp.int32) * MOE_TM, side='right'),
        N_EXPERTS - 1).astype(jnp.int32)
    n_used = (pad_end[-1] // MOE_TM).astype(jnp.int32).reshape(1)
    pos_blocked = pos.reshape(t // COMBINE_TM, COMBINE_TM, TOP_K).transpose(0, 2, 1).reshape(-1)

    x_rows = moe_gather(u, row_tok, n_used)
    act = moe_in(x_rows, blk_expert, n_used, w_in, wl)
    y_rows = moe_out(act, blk_expert, n_used, w_out, wl, row_w)
    return moe_combine(y_rows, pos_blocked, h, g_post, gate)


def _attn_kernel(*refs, scale, two, aliased):
    if aliased:
        refs = refs[1:]
    if two:
        q_ref, k_ref, v_ref, kc_ref, vc_ref, o_ref = refs
    else:
        q_ref, k_ref, v_ref, o_ref = refs
    q = q_ref[...].astype(BF16)
    s = lax.dot_general(q, k_ref[...].astype(BF16), NT, preferred_element_type=F32) * scale
    m = jnp.max(s, axis=-1, keepdims=True)
    if two:
        s2 = lax.dot_general(q, kc_ref[0].astype(BF16), NT, preferred_element_type=F32) * scale
        m = jnp.maximum(m, jnp.max(s2, axis=-1, keepdims=True))
    p = jnp.exp(s - m)
    l = jnp.sum(p, axis=-1, keepdims=True)
    o = jnp.dot(p.astype(BF16), v_ref[...].astype(BF16), preferred_element_type=F32)
    if two:
        p2 = jnp.exp(s2 - m)
        l = l + jnp.sum(p2, axis=-1, keepdims=True)
        o = o + jnp.dot(p2.astype(BF16), vc_ref[0].astype(BF16), preferred_element_type=F32)
    o_ref[...] = (o / l).astype(o_ref.dtype)


def attention(q_arr, k_arr, v_arr, *, n_batch, n_heads, nq, nk, tq, row0, q_col, k_col, v_col,
              k_cache=None, v_cache=None, kv_of_head=lambda h: h, into=None, name="attention"):
    hd = LANE
    nqb = nq // tq
    two = k_cache is not None
    aliased = into is not None
    in_specs = [pl.BlockSpec((tq, hd), lambda b, h, i: (row0 // tq + b * nqb + i, q_col(h))),
                pl.BlockSpec((nk, hd), lambda b, h, i: (row0 // nk + b, k_col(h))),
                pl.BlockSpec((nk, hd), lambda b, h, i: (row0 // nk + b, v_col(h)))]
    args = [q_arr, k_arr, v_arr]
    if two:
        n_past = k_cache.shape[1]
        in_specs += [pl.BlockSpec((1, n_past, hd), lambda b, h, i: (b, 0, kv_of_head(h))),
                     pl.BlockSpec((1, n_past, hd), lambda b, h, i: (b, 0, kv_of_head(h)))]
        args += [k_cache, v_cache]
    if aliased:
        in_specs = [pl.BlockSpec(memory_space=pl.ANY)] + in_specs
        args = [into] + args
    return pl.pallas_call(
        functools.partial(_attn_kernel, scale=hd ** -0.5, two=two, aliased=aliased),
        grid=(n_batch, n_heads, nqb),
        in_specs=in_specs,
        out_specs=pl.BlockSpec((tq, hd), lambda b, h, i: (row0 // tq + b * nqb + i, h)),
        out_shape=jax.ShapeDtypeStruct((N_TOK, n_heads * hd), BF16),
        input_output_aliases={0: 0} if aliased else {},
        compiler_params=_params(("parallel", "parallel", "arbitrary")),
        name=name,
    )(*args)


def _nat_latent_kernel(into_ref, q_ref, k_ref, v_ref, kc_ref, vc_ref, b_ref, o_ref, kb_ref, vb_ref, *, rows):
    del into_ref
    w, kh = GRID_W, NAT_KH
    nwin = kh * w
    half = kh // 2
    scale = NAT_HD ** -0.5
    qi = _iota((w, nwin), 0)
    kk = _iota((w, nwin), 1) % w
    c_start = jnp.clip(qi - NAT_KW // 2, 0, w - NAT_KW)
    col_ok = (kk >= c_start) & (kk < c_start + NAT_KW)
    kb_ref[...] = k_ref[...].astype(BF16)
    vb_ref[...] = v_ref[...].astype(BF16)
    kc = kc_ref[0].astype(BF16)
    vc = vc_ref[0].astype(BF16)

    def one_row(r, start, rel):
        q0 = r * w
        k0 = start * w
        if not isinstance(r, int):
            q0 = pl.multiple_of(q0, w)
            k0 = pl.multiple_of(k0, w)
        q = q_ref[pl.ds(q0, w), :].astype(BF16)
        s1 = lax.dot_general(q, kb_ref[pl.ds(k0, nwin), :], NT, preferred_element_type=F32) * scale + b_ref[0, rel]
        s1 = jnp.where(col_ok, s1, NEG)
        s2 = lax.dot_general(q, kc, NT, preferred_element_type=F32) * scale
        m = jnp.maximum(jnp.max(s1, axis=-1, keepdims=True), jnp.max(s2, axis=-1, keepdims=True))
        p1 = jnp.exp(s1 - m)
        p2 = jnp.exp(s2 - m)
        l = jnp.sum(p1, axis=-1, keepdims=True) + jnp.sum(p2, axis=-1, keepdims=True)
        o = (jnp.dot(p1.astype(BF16), vb_ref[pl.ds(k0, nwin), :], preferred_element_type=F32)
             + jnp.dot(p2.astype(BF16), vc, preferred_element_type=F32))
        o_ref[pl.ds(q0, w), :] = (o / l).astype(o_ref.dtype)

    last_start = rows - kh
    for r in range(half):
        one_row(r, 0, kh - 1 - r)

    def interior(r, c):
        one_row(r, r - half, kh - 1 - half)
        return c

    lax.fori_loop(half, last_start + half + 1, interior, 0)
    for r in range(last_start + half + 1, rows):
        one_row(r, last_start, last_start - r + kh - 1)


def nat_bias_slabs(rpb):
    c = jnp.arange(GRID_W)
    col_off = jnp.clip(c[None, :] - c[:, None], -(NAT_KW - 1), NAT_KW - 1) + NAT_KW - 1
    t = rpb[:, :, col_off]
    rel = jnp.arange(NAT_KH)[:, None] + jnp.arange(NAT_KH)[None, :]
    return t[:, rel].transpose(0, 1, 3, 2, 4).reshape(rpb.shape[0], NAT_KH, GRID_W, NAT_KH * GRID_W)


def nat_latent(qkv, k_cache, v_cache, bias, into):
    nh, hd = NAT_HEADS, NAT_HD
    rows = DEC_SEQ // GRID_W
    rb0 = N_PROMPT // DEC_SEQ
    blk = lambda c0: pl.BlockSpec((DEC_SEQ, hd), lambda b, h: (rb0 + b, c0 + h))
    return pl.pallas_call(
        functools.partial(_nat_latent_kernel, rows=rows),
        grid=(DEC_BATCH, nh),
        in_specs=[pl.BlockSpec(memory_space=pl.ANY), blk(0), blk(nh), blk(2 * nh),
                  pl.BlockSpec((1, PAST_LEN, hd), lambda b, h: (b, 0, h)),
                  pl.BlockSpec((1, PAST_LEN, hd), lambda b, h: (b, 0, h)),
                  pl.BlockSpec((1, NAT_KH, GRID_W, NAT_KH * GRID_W), lambda b, h: (h, 0, 0, 0))],
        out_specs=pl.BlockSpec((DEC_SEQ, hd), lambda b, h: (rb0 + b, h)),
        out_shape=jax.ShapeDtypeStruct((N_TOK, nh * hd), BF16),
        input_output_aliases={0: 0},
        scratch_shapes=[pltpu.VMEM((DEC_SEQ, hd), BF16), pltpu.VMEM((DEC_SEQ, hd), BF16)],
        compiler_params=_params(("parallel", "parallel")),
        name="nat_latent",
    )(into, qkv, qkv, qkv, k_cache, v_cache, bias)


def _qk_norm_rope_kernel(x_ref, g_ref, cos_ref, sa_ref, sb_ref, o_ref, *, n_heads):
    cos, sa, sb = cos_ref[...], sa_ref[...], sb_ref[...]
    quarter = GQA_HD // 4
    for hh in range(n_heads):
        sl = slice(hh * GQA_HD, (hh + 1) * GQA_HD)
        y = _rms(x_ref[:, sl], g_ref[:, sl])
        up = pltpu.roll(y, GQA_HD - quarter, axis=1)
        dn = pltpu.roll(y, quarter, axis=1)
        o_ref[:, sl] = y * cos + up * sa + dn * sb


def qk_norm_rope(qkv, gains, cos, sa, sb, *, n_heads, tm=512):
    m = qkv.shape[0]
    w = n_heads * GQA_HD
    return pl.pallas_call(
        functools.partial(_qk_norm_rope_kernel, n_heads=n_heads),
        grid=(m // tm,),
        in_specs=[pl.BlockSpec((tm, w), lambda i: (i, 0)),
                  pl.BlockSpec((1, w), lambda i: (0, 0)),
                  pl.BlockSpec((tm, GQA_HD), lambda i: (i, 0)),
                  pl.BlockSpec((tm, GQA_HD), lambda i: (i, 0)),
                  pl.BlockSpec((tm, GQA_HD), lambda i: (i, 0))],
        out_specs=pl.BlockSpec((tm, w), lambda i: (i, 0)),
        out_shape=jax.ShapeDtypeStruct((m, w), F32),
        compiler_params=_params(("parallel",)),
        name="qk_norm_rope",
    )(qkv, gains, cos, sa, sb)


def rope_tables():
    t = jnp.arange(DEC_SEQ)
    row = (t // GRID_W).astype(F32)
    col = (t % GRID_W).astype(F32)
    half = GQA_HD // 2
    freqs = ROPE_THETA ** (-jnp.arange(0, half, 2, dtype=F32) / half)
    ar = row[:, None] * freqs
    ac = col[:, None] * freqs
    z = jnp.zeros_like(ar)
    cos = jnp.concatenate([jnp.cos(ar), jnp.cos(ar), jnp.cos(ac), jnp.cos(ac)], axis=-1)
    sa = jnp.concatenate([-jnp.sin(ar), z, -jnp.sin(ac), z], axis=-1)
    sb = jnp.concatenate([z, jnp.sin(ar), z, jnp.sin(ac)], axis=-1)
    ones = jnp.ones((N_PROMPT, GQA_HD), F32)
    zeros = jnp.zeros((N_PROMPT, GQA_HD), F32)
    tile = lambda a: jnp.tile(a, (DEC_BATCH, 1))
    return (jnp.concatenate([ones, tile(cos)], 0), jnp.concatenate([zeros, tile(sa)], 0),
            jnp.concatenate([zeros, tile(sb)], 0))


def _mlstm_kernel(*refs, seq, zero_state):
    if zero_state:
        (q_ref, k_ref, v_ref, og_ref, gt_ref, bg_ref, gh_ref,
         a_ref, cf_ref, nf_ref, mf_ref, hf_ref, hb_ref, cs_ref, ns_ref, m_ref, y_ref, yt_ref) = refs
    else:
        (into_ref, q_ref, k_ref, v_ref, og_ref, gt_ref, bg_ref, gh_ref, c0_ref, n0_ref, m0_ref,
         a_ref, hf_ref, hb_ref, cs_ref, ns_ref, m_ref, y_ref, yt_ref) = refs
        del into_ref
    L = MLSTM_CHUNK
    nc = seq // L
    head = pl.program_id(1)
    scale = MLSTM_DQK ** -0.5

    if zero_state:
        cs_ref[...] = jnp.zeros_like(cs_ref)
        ns_ref[...] = jnp.zeros_like(ns_ref)
        m_ref[...] = jnp.full_like(m_ref, NEG)
    else:
        cs_ref[...] = c0_ref[0, :, 0]
        ns_ref[...] = n0_ref[0, :, 0]
        m_ref[...] = m0_ref[0, :, 0]

    rr = _iota((LANE, LANE), 0)
    cc = _iota((LANE, LANE), 1)
    sel = jnp.where((cc < 4) & (rr == MLSTM_HEADS * cc + head), 1.0, 0.0).astype(F32)
    lane = _iota((L, LANE), 1)
    jj = _iota((L, L), 0)
    ss = _iota((L, L), 1)
    lower = (ss <= jj)
    upper = (ss >= jj)
    lowerf = lower.astype(F32)

    def gate_terms(c, carry):
        c0 = pl.multiple_of(c * L, L)
        g = gt_ref[pl.ds(c0, L), :] + bg_ref[...]
        x = jnp.dot(g, sel, preferred_element_type=F32, precision=HIGHEST)
        logsig = jnp.minimum(x, 0.0) - jnp.log(1.0 + jnp.exp(-jnp.abs(x)))
        x = jnp.where((lane == 1) | (lane == 3), logsig, x)
        pre = jnp.dot(lowerf, x, preferred_element_type=F32, precision=HIGHEST)
        suf = pre[L - 1:L, :] - pre + x
        y = jnp.where(lane == 1, pre, jnp.where(lane == 3, suf, x))
        y_ref[pl.ds(c0, L), :] = y
        yt_ref[pl.ds(pl.multiple_of(c * 8, 8), 8), :] = y.T[:8, :]
        return carry

    lax.fori_loop(0, nc, gate_terms, 0, unroll=4)

    def direction(d, c, c0):
        y = y_ref[pl.ds(c0, L), :]
        yt = yt_ref[pl.ds(pl.multiple_of(c * 8, 8), 8), :]
        li_col, b_col = y[:, 2 * d:2 * d + 1], y[:, 2 * d + 1:2 * d + 2]
        li_row, b_row = yt[2 * d:2 * d + 1, :], yt[2 * d + 1:2 * d + 2, :]
        bl = b_col[L - 1:L, :] if d == 0 else b_col[0:1, :]
        m = m_ref[d][:, :1]
        dlog = jnp.where(lower if d == 0 else upper, b_col - b_row + li_row, -jnp.inf)
        inter = b_col + m
        mj = jnp.maximum(inter, jnp.max(dlog, axis=-1, keepdims=True))
        dw = jnp.exp(dlog - mj)
        iw = jnp.exp(inter - mj)
        qc = q_ref[pl.ds(c0, L), :] * scale
        kc = k_ref[pl.ds(c0, L), :]
        qb = qc.astype(BF16)
        vb = v_ref[pl.ds(c0, L), :].astype(BF16)
        sc = lax.dot_general(qb, kc.astype(BF16), NT, preferred_element_type=F32) * dw
        cs = cs_ref[d]
        ns = ns_ref[d]
        num = (jnp.dot(sc.astype(BF16), vb, preferred_element_type=F32)
               + iw * jnp.dot(qb, cs.astype(BF16), preferred_element_type=F32))
        den = jnp.sum(sc, axis=-1, keepdims=True) + iw * jnp.sum(qc * ns, axis=-1, keepdims=True)
        hc = num / jnp.maximum(jnp.abs(den), jnp.exp(-mj))
        elog = bl - b_col + li_col
        carry_log = bl + m
        m_new = jnp.maximum(carry_log, jnp.max(elog, axis=0, keepdims=True))
        ew = jnp.exp(elog - m_new)
        cw = jnp.exp(carry_log - m_new)
        kw = ew * kc
        cs_ref[d] = cw * cs + lax.dot_general(kw.astype(BF16), vb, TN, preferred_element_type=F32)
        ns_ref[d] = cw * ns + jnp.sum(kw, axis=0, keepdims=True)
        m_ref[d] = jnp.broadcast_to(m_new, (1, LANE))
        return hc

    def body(c, carry):
        cf = pl.multiple_of(c * L, L)
        cb = pl.multiple_of((nc - 1 - c) * L, L)
        hf_ref[pl.ds(cf, L), :] = direction(0, c, cf)
        hb_ref[pl.ds(cb, L), :] = direction(1, nc - 1 - c, cb)
        return carry

    lax.fori_loop(0, nc, body, 0, unroll=2)
    hsum = _rms(hf_ref[...] + hb_ref[...], gh_ref[...])
    a_ref[...] = (hsum * jax.nn.sigmoid(og_ref[...])).astype(a_ref.dtype)
    if zero_state:
        cf_ref[0, :, 0] = cs_ref[...]
        nf_ref[0, :, 0] = ns_ref[...]
        mf_ref[0, :, 0] = m_ref[...]


def mlstm_scan(proj, gates, b_gate, g_head, *, n_batch, seq, row0, state=None, into=None):
    nh, dqk, dv = MLSTM_HEADS, MLSTM_DQK, MLSTM_DV
    rb0 = row0 // seq
    zero_state = state is None
    in_specs = [pl.BlockSpec((seq, dqk), lambda b, h: (rb0 + b, h)),
                pl.BlockSpec((seq, dqk), lambda b, h: (rb0 + b, nh + h)),
                pl.BlockSpec((seq, dv), lambda b, h: (rb0 + b, nh + h)),
                pl.BlockSpec((seq, dv), lambda b, h: (rb0 + b, 2 * nh + h)),
                pl.BlockSpec((seq, LANE), lambda b, h: (rb0 + b, 0)),
                pl.BlockSpec((1, LANE), lambda b, h: (0, 0)),
                pl.BlockSpec((1, dv), lambda b, h: (0, h))]
    bg = jnp.zeros((1, LANE), F32).at[0, :4 * nh].set(b_gate)
    args = [proj, proj, proj, proj, gates, bg, g_head.reshape(1, nh * dv)]
    out_specs = [pl.BlockSpec((seq, dv), lambda b, h: (rb0 + b, h))]
    out_shape = [jax.ShapeDtypeStruct((N_TOK, nh * dv), BF16)]
    c_spec = pl.BlockSpec((1, 2, 1, dqk, dv), lambda b, h: (b, 0, h, 0, 0))
    n_spec = pl.BlockSpec((1, 2, 1, 1, dqk), lambda b, h: (b, 0, h, 0, 0))
    aliases = {}
    if zero_state:
        out_specs += [c_spec, n_spec, n_spec]
        out_shape += [jax.ShapeDtypeStruct((n_batch, 2, nh, dqk, dv), F32),
                      jax.ShapeDtypeStruct((n_batch, 2, nh, 1, dqk), F32),
                      jax.ShapeDtypeStruct((n_batch, 2, nh, 1, LANE), F32)]
    else:
        c0, n0, m0 = state
        in_specs = [pl.BlockSpec(memory_space=pl.ANY)] + in_specs + [c_spec, n_spec, n_spec]
        args = [into] + args + [c0, n0.reshape(n_batch, 2, nh, 1, dqk),
                                jnp.broadcast_to(m0[..., None, None], (n_batch, 2, nh, 1, LANE))]
        aliases = {0: 0}
    return pl.pallas_call(
        functools.partial(_mlstm_kernel, seq=seq, zero_state=zero_state),
        grid=(n_batch, nh),
        in_specs=in_specs,
        out_specs=out_specs,
        out_shape=out_shape,
        input_output_aliases=aliases,
        scratch_shapes=[pltpu.VMEM((seq, dv), F32), pltpu.VMEM((seq, dv), F32),
                        pltpu.VMEM((2, dqk, dv), F32), pltpu.VMEM((2, 1, dqk), F32), pltpu.VMEM((2, 1, LANE), F32),
                        pltpu.VMEM((seq, LANE), F32), pltpu.VMEM((seq // MLSTM_CHUNK * 8, MLSTM_CHUNK), F32)],
        compiler_params=_params(("parallel", "parallel")),
        name="mlstm_scan_ctx" if zero_state else "mlstm_scan_lat",
    )(*args)


def _conv_silu_kernel(*refs, seq, aliased):
    if aliased:
        refs = refs[1:]
    x_ref, w_ref, b_ref, o_ref = refs
    x = x_ref[...]
    t = _iota((seq, 1), 0)
    half = SSD_CONV // 2
    acc = b_ref[...] + w_ref[half:half + 1, :] * x
    for k in range(SSD_CONV):
        off = k - half
        if off == 0:
            continue
        shifted = pltpu.roll(x, (-off) % seq, axis=0)
        valid = (t + off >= 0) & (t + off < seq)
        acc = acc + w_ref[k:k + 1, :] * jnp.where(valid, shifted, 0.0)
    o_ref[...] = acc * jax.nn.sigmoid(acc)


def conv_silu(proj, conv_w, conv_b, *, n_batch, seq, row0, col0, into=None, tc=512):
    ch = conv_w.shape[1]
    rb0 = row0 // seq
    aliased = into is not None
    in_specs = [pl.BlockSpec((seq, tc), lambda b, j: (rb0 + b, col0 // tc + j)),
                pl.BlockSpec((SSD_CONV, tc), lambda b, j: (0, j)),
                pl.BlockSpec((1, tc), lambda b, j: (0, j))]
    args = [proj, conv_w, conv_b.reshape(1, ch)]
    if aliased:
        in_specs = [pl.BlockSpec(memory_space=pl.ANY)] + in_specs
        args = [into] + args
    return pl.pallas_call(
        functools.partial(_conv_silu_kernel, seq=seq, aliased=aliased),
        grid=(n_batch, ch // tc),
        in_specs=in_specs,
        out_specs=pl.BlockSpec((seq, tc), lambda b, j: (rb0 + b, j)),
        out_shape=jax.ShapeDtypeStruct((N_TOK, ch), F32),
        input_output_aliases={0: 0} if aliased else {},
        compiler_params=_params(("parallel", "parallel")),
        name="ssd_conv",
    )(*args)


def _ssd_kernel(*refs, seq, zero_state):
    if zero_state:
        (xs_ref, bm_ref, cm_ref, dt_ref, z_ref, par_ref, dsk_ref,
         y_ref, ssq_ref, sf_ref, yb_ref, st_ref) = refs
    else:
        (into_y, into_q, xs_ref, bm_ref, cm_ref, dt_ref, z_ref, par_ref, dsk_ref, s0_ref,
         y_ref, ssq_ref, yb_ref, st_ref) = refs
        del into_y, into_q
    L = SSD_CHUNK
    nc = seq // L
    R, P = SSD_R, SSD_HEADDIM
    grp = pl.program_id(1)

    if zero_state:
        st_ref[...] = jnp.zeros_like(st_ref)
    else:
        st_ref[...] = s0_ref[0]

    rr = _iota((LANE, LANE), 0)
    cc = _iota((LANE, LANE), 1)
    sel = jnp.where((cc < 2 * R) & (rr == (cc // R) * SSD_HEADS + grp * R + cc % R), 1.0, 0.0).astype(F32)
    lane = _iota((L, LANE), 1)
    ll = _iota((L, L), 0)
    ss = _iota((L, L), 1)
    lower = (ss <= ll)
    upper = (ss >= ll)
    lowerf = lower.astype(F32)
    dt_bias = par_ref[0, 0:1, :]
    a_neg = -jnp.exp(par_ref[0, 1:2, :])

    def direction(d, c0, out_ref):
        dtc = _softplus(jnp.dot(dt_ref[pl.ds(c0, L), :], sel, preferred_element_type=F32, precision=HIGHEST)
                        + dt_bias)
        dac = dtc * a_neg
        pre = jnp.dot(lowerf, dac, preferred_element_type=F32, precision=HIGHEST)
        tot = pre[L - 1:L, :]
        acum = pre if d == 0 else tot - pre + dac
        acum_t = acum.T
        dtc_t = dtc.T
        wdec = dtc * jnp.exp(tot - acum)
        bmc = bm_ref[pl.ds(c0, L), :].astype(BF16)
        cmf = cm_ref[pl.ds(c0, L), :]
        cb = lax.dot_general(cmf.astype(BF16), bmc, NT, preferred_element_type=F32)
        mask = lower if d == 0 else upper
        for r in range(R):
            u = d * R + r
            colb = jnp.broadcast_to(acum[:, u:u + 1], (L, L))
            lmat = jnp.exp(jnp.where(mask, colb - acum_t[u:u + 1, :], -jnp.inf))
            xs = xs_ref[pl.ds(c0, L), r * P:(r + 1) * P]
            y_diag = jnp.dot((cb * lmat * dtc_t[u:u + 1, :]).astype(BF16), xs.astype(BF16),
                             preferred_element_type=F32)
            start = st_ref[d, r]
            y_off = lax.dot_general((cmf * jnp.exp(colb)).astype(BF16), start.astype(BF16), NT,
                                    preferred_element_type=F32)
            new = lax.dot_general((xs * wdec[:, u:u + 1]).astype(BF16), bmc, TN, preferred_element_type=F32)
            st_ref[d, r] = jnp.exp(tot[:, u:u + 1]) * start + new
            out_ref[pl.ds(c0, L), r * P:(r + 1) * P] = y_diag + y_off

    def body(c, carry):
        cf = pl.multiple_of(c * L, L)
        cbk = pl.multiple_of((nc - 1 - c) * L, L)
        direction(0, cf, y_ref)
        direction(1, cbk, yb_ref)
        return carry

    lax.fori_loop(0, nc, body, 0)
    z = z_ref[...]
    y = (y_ref[...] + yb_ref[...] + dsk_ref[...] * xs_ref[...]) * (z * jax.nn.sigmoid(z))
    y_ref[...] = y
    part = jnp.broadcast_to(jnp.sum(y * y, axis=-1, keepdims=True), ssq_ref.shape)

    @pl.when(grp == 0)
    def _():
        ssq_ref[...] = part

    @pl.when(grp > 0)
    def _():
        ssq_ref[...] += part

    if zero_state:
        sf_ref[0] = st_ref[...]


def ssd_scan(proj, xbc, dt_bias, a_log, d_skip, *, n_batch, seq, row0, state=None, into=None):
    di, gw, ng, r = SSD_D_INNER, SSD_GW, SSD_GROUPS, SSD_R
    rb0 = row0 // seq
    zero_state = state is None
    dt_blk = (2 * di + 2 * ng * SSD_STATE) // LANE
    pack = lambda p: p.reshape(2, ng, r).transpose(1, 0, 2).reshape(ng, 2 * r)
    par = jnp.zeros((ng, 8, LANE), F32).at[:, 0, :2 * r].set(pack(dt_bias)).at[:, 1, :2 * r].set(pack(a_log))
    dsk = jnp.repeat(d_skip, SSD_HEADDIM).reshape(1, di)
    in_specs = [pl.BlockSpec((seq, gw), lambda b, g: (rb0 + b, g)),
                pl.BlockSpec((seq, SSD_STATE), lambda b, g: (rb0 + b, di // SSD_STATE + g)),
                pl.BlockSpec((seq, SSD_STATE), lambda b, g: (rb0 + b, di // SSD_STATE + ng + g)),
                pl.BlockSpec((seq, LANE), lambda b, g: (rb0 + b, dt_blk)),
                pl.BlockSpec((seq, gw), lambda b, g: (rb0 + b, g)),
                pl.BlockSpec((1, 8, LANE), lambda b, g: (g, 0, 0)),
                pl.BlockSpec((1, gw), lambda b, g: (0, g))]
    args = [xbc, xbc, xbc, proj, proj, par, dsk]
    out_specs = [pl.BlockSpec((seq, gw), lambda b, g: (rb0 + b, g)),
                 pl.BlockSpec((seq, LANE), lambda b, g: (rb0 + b, 0))]
    out_shape = [jax.ShapeDtypeStruct((N_TOK, di), F32), jax.ShapeDtypeStruct((N_TOK, LANE), F32)]
    s_spec = pl.BlockSpec((1, 2, r, SSD_HEADDIM, SSD_STATE), lambda b, g: (b, 0, g, 0, 0))
    aliases = {}
    if zero_state:
        out_specs.append(s_spec)
        out_shape.append(jax.ShapeDtypeStruct((n_batch, 2, SSD_HEADS, SSD_HEADDIM, SSD_STATE), F32))
    else:
        in_specs = [pl.BlockSpec(memory_space=pl.ANY)] * 2 + in_specs + [s_spec]
        args = list(into) + args + [state]
        aliases = {0: 0, 1: 1}
    return pl.pallas_call(
        functools.partial(_ssd_kernel, seq=seq, zero_state=zero_state),
        grid=(n_batch, ng),
        in_specs=in_specs,
        out_specs=out_specs,
        out_shape=out_shape,
        input_output_aliases=aliases,
        scratch_shapes=[pltpu.VMEM((seq, gw), F32), pltpu.VMEM((2, r, SSD_HEADDIM, SSD_STATE), F32)],
        compiler_params=_params(("parallel", "arbitrary")),
        name="ssd_scan_ctx" if zero_state else "ssd_scan_lat",
    )(*args)


def kernel(x_prompt, x_sample, state_mlstm_c, state_mlstm_n, state_mlstm_m, cache_nat_k, cache_nat_v, cache_gqa_k, cache_gqa_v, state_ssd, c, c_ctx, w_mod, b_mod, norm_g, mlstm_w_in, mlstm_b_gate, mlstm_g_head, mlstm_w_out, nat_w_qkv, nat_rpb, nat_w_out, gqa_w_qkv, gqa_q_g, gqa_k_g, gqa_w_out, ssd_w_in, ssd_conv_w, ssd_conv_b, ssd_dt_bias, ssd_a_log, ssd_d, ssd_g_norm, ssd_w_out, ffn_w_in, ffn_w_out, moe_w_router, moe_b_router, moe_w_in, moe_w_out):
    d = D_MODEL
    h = jnp.concatenate([x_prompt.reshape(N_PROMPT, d), x_sample.reshape(N_LATENT, d)], axis=0)
    cvecs = jnp.zeros((8, d), F32).at[0].set(c_ctx).at[1:1 + DEC_BATCH].set(c)
    mod = modulation_all(cvecs, w_mod, b_mod)[:, :N_GROUPS].reshape(DEPTH, N_GROUPS, 6, 1, d)
    outs = {}
    for layer in range(DEPTH):
        kind, j = layer % N_MIXERS, layer // N_MIXERS
        sh1, sc1, g1, sh2, sc2, g2 = [mod[layer, :, i] for i in range(6)]
        ng = norm_g[layer]
        if kind == 0:
            n_main = 2 * MLSTM_HEADS * MLSTM_DQK + 2 * MLSTM_HEADS * MLSTM_DV
            proj = norm_linear(h, ng[0], sc1, sh1, mlstm_w_in, j, n_out=n_main, name="mlstm_in")
            w_gates = jnp.zeros((1, d, LANE), F32).at[0, :, :4 * MLSTM_HEADS].set(mlstm_w_in[j, :, n_main:])
            gates = norm_linear(h, ng[0], sc1, sh1, w_gates, 0, tn=LANE, name="mlstm_gates")
            a, st_c, st_n, st_m = mlstm_scan(proj, gates, mlstm_b_gate[j], mlstm_g_head[j],
                                             n_batch=BATCH, seq=SEQ, row0=0)
            a = mlstm_scan(proj, gates, mlstm_b_gate[j], mlstm_g_head[j], n_batch=DEC_BATCH, seq=DEC_SEQ,
                           row0=N_PROMPT, state=(state_mlstm_c[:, j], state_mlstm_n[:, j], state_mlstm_m[:, j]),
                           into=a)[0]
            outs['mc'] = st_c[:, None]
            outs['mn'] = st_n[:, None, :, :, 0]
            outs['mm'] = st_m[:, None, :, :, 0, 0]
            h = linear_out(a, mlstm_w_out, j, h, ng[1], g1, name="mlstm_out")
        elif kind == 1:
            nh = NAT_HEADS
            hw = nh * NAT_HD
            qkv = norm_linear(h, ng[0], sc1, sh1, nat_w_qkv, j, name="nat_qkv")
            a = attention(qkv, qkv, qkv, n_batch=BATCH, n_heads=nh, nq=SEQ, nk=SEQ, tq=SEQ, row0=0,
                          q_col=lambda hh: hh, k_col=lambda hh: nh + hh, v_col=lambda hh: 2 * nh + hh,
                          name="nat_ctx_attn")
            a = nat_latent(qkv, cache_nat_k[:, j].reshape(DEC_BATCH, PAST_LEN, hw),
                           cache_nat_v[:, j].reshape(DEC_BATCH, PAST_LEN, hw), nat_bias_slabs(nat_rpb[j]), a)
            outs['nk'] = qkv[:N_PROMPT, hw:2 * hw].reshape(BATCH, 1, SEQ, nh, NAT_HD)
            outs['nv'] = qkv[:N_PROMPT, 2 * hw:].reshape(BATCH, 1, SEQ, nh, NAT_HD)
            h = linear_out(a, nat_w_out, j, h, ng[1], g1, name="nat_out")
        elif kind == 2:
            nqk = GQA_HEADS + GQA_KV_HEADS
            grp = GQA_HEADS // GQA_KV_HEADS
            kw = GQA_KV_HEADS * GQA_HD
            qkv = norm_linear(h, ng[0], sc1, sh1, gqa_w_qkv, j, name="gqa_qkv")
            gains = jnp.concatenate([jnp.tile(gqa_q_g[j], GQA_HEADS), jnp.tile(gqa_k_g[j], GQA_KV_HEADS)])
            cos, sa, sb = rope_tables()
            qk = qk_norm_rope(qkv, gains.reshape(1, -1), cos, sa, sb, n_heads=nqk)
            cols = dict(q_col=lambda hh: hh, k_col=lambda hh: GQA_HEADS + hh // grp,
                        v_col=lambda hh: nqk + hh // grp)
            a = attention(qk, qk, qkv, n_batch=BATCH, n_heads=GQA_HEADS, nq=SEQ, nk=SEQ, tq=SEQ, row0=0,
                          name="gqa_ctx_attn", **cols)
            a = attention(qk, qk, qkv, n_batch=DEC_BATCH, n_heads=GQA_HEADS, nq=DEC_SEQ, nk=DEC_SEQ, tq=256,
                          row0=N_PROMPT, k_cache=cache_gqa_k[:, j].reshape(DEC_BATCH, PAST_LEN, kw),
                          v_cache=cache_gqa_v[:, j].reshape(DEC_BATCH, PAST_LEN, kw),
                          kv_of_head=lambda hh: hh // grp, into=a, name="gqa_lat_attn", **cols)
            outs['gk'] = qk[:N_PROMPT, GQA_HEADS * GQA_HD:].reshape(BATCH, 1, SEQ, GQA_KV_HEADS, GQA_HD)
            outs['gv'] = qkv[:N_PROMPT, GQA_HEADS * GQA_HD + kw:].reshape(BATCH, 1, SEQ, GQA_KV_HEADS, GQA_HD)
            h = linear_out(a, gqa_w_out, j, h, ng[1], g1, name="gqa_out")
        else:
            proj = norm_linear(h, ng[0], sc1, sh1, ssd_w_in, j, name="ssd_in")
            conv = dict(col0=SSD_D_INNER)
            xbc = conv_silu(proj, ssd_conv_w[j], ssd_conv_b[j], n_batch=BATCH, seq=SEQ, row0=0, **conv)
            xbc = conv_silu(proj, ssd_conv_w[j], ssd_conv_b[j], n_batch=DEC_BATCH, seq=DEC_SEQ, row0=N_PROMPT,
                            into=xbc, **conv)
            y, ssq, st_s = ssd_scan(proj, xbc, ssd_dt_bias[j], ssd_a_log[j], ssd_d[j],
                                    n_batch=BATCH, seq=SEQ, row0=0)
            y, ssq = ssd_scan(proj, xbc, ssd_dt_bias[j], ssd_a_log[j], ssd_d[j], n_batch=DEC_BATCH, seq=DEC_SEQ,
                              row0=N_PROMPT, state=state_ssd[:, j], into=(y, ssq))
            outs['ss'] = st_s[:, None]
            h = linear_out(y, ssd_w_out, j, h, ng[1], g1, ssq=ssq, gn=ssd_g_norm[j], name="ssd_out")
        e = layer // 2
        if layer % 2 == 0:
            act = norm_swiglu_in(h, ng[2], sc2, sh2, ffn_w_in, e)
            h = linear_out(act, ffn_w_out, e, h, ng[3], g2, name="ffn_out")
        else:
            h = moe_layer(h, ng[2], sc2, sh2, moe_w_router[e], moe_b_router[e], moe_w_in, moe_w_out, e,
                          ng[3], g2)
    y_prompt = h[:N_PROMPT].reshape(BATCH, SEQ, d)
    y_sample = h[N_PROMPT:].reshape(DEC_BATCH, DEC_SEQ, d)
    return (y_prompt, y_sample, outs['mc'], outs['mn'], outs['mm'], outs['nk'], outs['nv'],
            outs['gk'], outs['gv'], outs['ss'])
```
